```python
import math
import jax, jax.numpy as jnp
from jax import lax
import numpy as np

D_MODEL = 1024
BATCH = 8
SEQ = 4096
DEPTH = 4

GRID_W = 64
NA_HEADS = 8
NA_HEAD_DIM = 64
NA_WIDTH = NA_HEADS * NA_HEAD_DIM
NA_WIN_H = 8
NA_WIN_W = 16
SSD_HEADS = 16
SSD_HEAD_DIM = 64
SSD_WIDTH = SSD_HEADS * SSD_HEAD_DIM
SSD_GROUPS = 2
SSD_HEADS_PER_GROUP = SSD_HEADS // SSD_GROUPS
SSD_STATE = 128
SSD_CONV = 5
SSD_CHUNK = 128
SSD_CONV_DIM = SSD_WIDTH + 2 * SSD_GROUPS * SSD_STATE
FNET_GROUPS = 4
FNET_GROUP_DIM = 128
FNET_WIDTH = FNET_GROUPS * FNET_GROUP_DIM
D_MIX = NA_WIDTH + SSD_WIDTH + FNET_WIDTH
IN_SPLITS = (NA_WIDTH, NA_WIDTH, NA_WIDTH, NA_WIDTH,
             SSD_WIDTH, SSD_CONV_DIM, 2 * SSD_HEADS,
             FNET_WIDTH, FNET_WIDTH)
IN_COLS = sum(IN_SPLITS)
IN_SPLIT_POINTS = tuple(int(v) for v in np.cumsum(IN_SPLITS)[:-1])
RMS_EPS = 1e-6

kernel_name = 'hymba_style_natten_ssd_fnet_encoder'


def rms_norm(x, w):
    xf = x.astype(jnp.float32)
    y = xf * lax.rsqrt(jnp.mean(xf * xf, axis=-1, keepdims=True) + RMS_EPS)
    return (y * w.astype(jnp.float32)).astype(x.dtype)


def neighbourhood_attention(q, k, v, rpb):
    b, s, h, dh = q.shape
    rows = s // GRID_W
    wh = min(NA_WIN_H, rows)
    ww = NA_WIN_W
    qg = q.reshape(b, rows, GRID_W, h, dh)
    kg = k.reshape(b, rows, GRID_W, h, dh)
    vg = v.reshape(b, rows, GRID_W, h, dh)
    row_ids = jnp.arange(rows)
    row_start = jnp.clip(row_ids - wh // 2, 0, rows - wh)
    col_ids = jnp.arange(GRID_W)
    col_start = jnp.clip(col_ids - ww // 2, 0, GRID_W - ww)
    col_idx = col_start[:, None] + jnp.arange(ww)[None, :]
    col_rel = col_idx - col_ids[:, None] + (NA_WIN_W - 1)
    bias_cols = rpb[:, :, col_rel].astype(jnp.float32)
    scale = dh ** -0.5

    def one_row(args):
        q_r, r, r0 = args
        k_band = lax.dynamic_slice_in_dim(kg, r0, wh, axis=1)
        v_band = lax.dynamic_slice_in_dim(vg, r0, wh, axis=1)
        k_win = k_band[:, :, col_idx]
        v_win = v_band[:, :, col_idx]
        row_rel = r0 + jnp.arange(wh) - r + (NA_WIN_H - 1)
        bias = jnp.take(bias_cols, row_rel, axis=1).transpose(0, 2, 1, 3)
        logits = jnp.einsum('bqhd,bwqkhd->bhqwk', q_r, k_win).astype(jnp.float32) * scale + bias[None]
        p = jax.nn.softmax(logits.reshape(b, h, GRID_W, wh * ww), axis=-1)
        p = p.reshape(b, h, GRID_W, wh, ww).astype(v.dtype)
        return jnp.einsum('bhqwk,bwqkhd->bqhd', p, v_win)

    out = lax.map(one_row, (jnp.moveaxis(qg, 1, 0), row_ids, row_start))
    return jnp.moveaxis(out, 0, 1).reshape(b, s, h * dh)


def centred_depthwise_conv(u, w, bias):
    s = u.shape[1]
    pad = SSD_CONV // 2
    up = jnp.pad(u, ((0, 0), (pad, pad), (0, 0)))
    out = bias + up[:, 0:s] * w[0]
    for j in range(1, SSD_CONV):
        out = out + up[:, j:j + s] * w[j]
    return out


def segsum(a):
    l = a.shape[-1]
    cs = jnp.cumsum(a, axis=-1)
    diff = cs[..., :, None] - cs[..., None, :]
    mask = jnp.tril(jnp.ones((l, l), dtype=bool))
    return jnp.where(mask, diff, -jnp.inf)


def ssd_scan(x, dt, a, bm, cm):
    b, s, g, kh, p = x.shape
    n = bm.shape[-1]
    l = SSD_CHUNK
    c = s // l
    xd = (x * dt[..., None]).reshape(b, c, l, g, kh, p)
    adt = (dt * a).reshape(b, c, l, g, kh).transpose(0, 3, 4, 1, 2)
    a_cs = jnp.cumsum(adt, axis=-1)
    bc = bm.reshape(b, c, l, g, n)
    cc = cm.reshape(b, c, l, g, n)
    decay_in = jnp.exp(segsum(adt))
    y_diag = jnp.einsum('bclgn,bcsgn,bgkcls,bcsgkp->bclgkp', cc, bc, decay_in, xd)
    decay_states = jnp.exp(a_cs[..., -1:] - a_cs)
    states = jnp.einsum('bclgn,bgkcl,bclgkp->cbgkpn', bc, decay_states, xd)
    chunk_decay = jnp.moveaxis(jnp.exp(a_cs[..., -1]), -1, 0)

    def step(hst, inp):
        st, dec = inp
        return hst * dec[..., None, None] + st, hst

    h0 = jnp.zeros((b, g, kh, p, n), dtype=states.dtype)
    _, prev = lax.scan(step, h0, (states, chunk_decay))
    y_off = jnp.einsum('bclgn,cbgkpn,bgkcl->bclgkp', cc, prev, jnp.exp(a_cs))
    return (y_diag + y_off).reshape(b, s, g, kh, p)


def ssd_mixer(z, xbc, dt_raw, conv_w, conv_b, dt_bias, a_log, d_skip, norm_w):
    b, s, _ = z.shape
    g, kh, p, n = SSD_GROUPS, SSD_HEADS_PER_GROUP, SSD_HEAD_DIM, SSD_STATE
    xbc = jax.nn.silu(centred_depthwise_conv(xbc, conv_w, conv_b))
    xs, bm, cm = jnp.split(xbc, [SSD_WIDTH, SSD_WIDTH + g * n], axis=-1)
    x = xs.reshape(b, s, g, kh, p)
    bm = bm.reshape(b, s, g, n)
    cm = cm.reshape(b, s, g, n)
    dt = jax.nn.softplus(dt_raw.astype(jnp.float32) + dt_bias.reshape(-1).astype(jnp.float32))
    dt = dt.reshape(b, s, 2, g, kh)
    a = -jnp.exp(a_log.astype(jnp.float32)).reshape(2, g, kh)
    flip = lambda t: jnp.flip(t, axis=1)
    y_f = ssd_scan(x, dt[:, :, 0], a[0], bm, cm)
    y_b = flip(ssd_scan(flip(x), flip(dt[:, :, 1]), a[1], flip(bm), flip(cm)))
    y = y_f + y_b + x * d_skip.reshape(g, kh)[..., None]
    y = y.reshape(b, s, SSD_WIDTH).astype(jnp.float32) * jax.nn.silu(z.astype(jnp.float32))
    yg = y.reshape(b, s, g, SSD_WIDTH // g)
    yg = yg * lax.rsqrt(jnp.mean(yg * yg, axis=-1, keepdims=True) + RMS_EPS)
    y = yg.reshape(b, s, SSD_WIDTH) * norm_w.astype(jnp.float32)
    return y.astype(z.dtype)


def fourier_mixer(u, gate, w_f):
    b, s, _ = u.shape
    ug = u.reshape(b, s, FNET_GROUPS, FNET_GROUP_DIM).astype(jnp.float32)
    mixed = jnp.fft.fft2(ug, axes=(1, 3), norm='ortho').real
    y = jnp.einsum('bsgc,gcd->bsgd', mixed, w_f.astype(jnp.float32)).reshape(b, s, FNET_WIDTH)
    return (y * jax.nn.silu(gate.astype(jnp.float32))).astype(u.dtype)


def hybrid_layer(x, norm_w, w_in, rpb, conv_w, conv_b, dt_bias, a_log, d_skip, ssd_norm_w, w_fourier, w_out):
    b, s, _ = x.shape
    h = rms_norm(x, norm_w)
    proj = jnp.einsum('bsd,de->bse', h, w_in)
    q, k, v, g_na, z, xbc, dt_raw, f_in, g_f = jnp.split(proj, IN_SPLIT_POINTS, axis=-1)
    heads = lambda t: t.reshape(b, s, NA_HEADS, NA_HEAD_DIM)
    na = neighbourhood_attention(heads(q), heads(k), heads(v), rpb) * jax.nn.silu(g_na)
    ssd = ssd_mixer(z, xbc, dt_raw, conv_w, conv_b, dt_bias, a_log, d_skip, ssd_norm_w)
    fn = fourier_mixer(f_in, g_f, w_fourier)
    mixed = jnp.concatenate([na.astype(x.dtype), ssd.astype(x.dtype), fn.astype(x.dtype)], axis=-1)
    return x + jnp.einsum('bse,ed->bsd', mixed, w_out).astype(x.dtype)


def setup_inputs(seed: int = 0) -> dict:
    key = jax.random.key(seed)
    ks = jax.random.split(key, 13)
    f32 = jnp.float32
    nrm = lambda kk, shp: jax.random.normal(kk, shp, f32)
    x = nrm(ks[0], (BATCH, SEQ, D_MODEL))
    norm_w = 1.0 + 0.02 * nrm(ks[1], (DEPTH, D_MODEL))
    w_in = nrm(ks[2], (DEPTH, D_MODEL, IN_COLS)) * D_MODEL ** -0.5
    na_rpb = 0.1 * nrm(ks[3], (DEPTH, NA_HEADS, 2 * NA_WIN_H - 1, 2 * NA_WIN_W - 1))
    conv_w = nrm(ks[4], (DEPTH, SSD_CONV, SSD_CONV_DIM)) * SSD_CONV ** -0.5
    conv_b = 0.02 * nrm(ks[5], (DEPTH, SSD_CONV_DIM))
    dt0 = jnp.exp(jax.random.uniform(ks[6], (DEPTH, 2, SSD_HEADS), f32,
                                     minval=math.log(1e-3), maxval=math.log(1e-1)))
    dt_bias = dt0 + jnp.log(-jnp.expm1(-dt0))
    a_log = jnp.log(jax.random.uniform(ks[7], (DEPTH, 2, SSD_HEADS), f32, minval=1.0, maxval=16.0))
    d_skip = 1.0 + 0.1 * nrm(ks[8], (DEPTH, SSD_HEADS))
    ssd_norm_w = 1.0 + 0.02 * nrm(ks[9], (DEPTH, SSD_WIDTH))
    w_fourier = nrm(ks[10], (DEPTH, FNET_GROUPS, FNET_GROUP_DIM, FNET_GROUP_DIM)) * FNET_GROUP_DIM ** -0.5
    w_out = nrm(ks[11], (DEPTH, D_MIX, D_MODEL)) * D_MIX ** -0.5
    final_norm_w = 1.0 + 0.02 * nrm(ks[12], (D_MODEL,))
    return {'x': x, 'norm_w': norm_w, 'w_in': w_in, 'na_rpb': na_rpb, 'conv_w': conv_w,
            'conv_b': conv_b, 'dt_bias': dt_bias, 'a_log': a_log, 'd_skip': d_skip,
            'ssd_norm_w': ssd_norm_w, 'w_fourier': w_fourier, 'w_out': w_out,
            'final_norm_w': final_norm_w}


def reference(x, norm_w, w_in, na_rpb, conv_w, conv_b, dt_bias, a_log, d_skip, ssd_norm_w,
              w_fourier, w_out, final_norm_w):
    for i in range(DEPTH):
        x = hybrid_layer(x, norm_w[i], w_in[i], na_rpb[i], conv_w[i], conv_b[i], dt_bias[i],
                         a_log[i], d_skip[i], ssd_norm_w[i], w_fourier[i], w_out[i])
    return rms_norm(x, final_norm_w)
```

```python
import functools
import math

import numpy as np
import jax
import jax.numpy as jnp
from jax import lax
from jax.experimental import pallas as pl
from jax.experimental.pallas import tpu as pltpu

F32 = jnp.float32
BF16 = jnp.bfloat16

D_MODEL = 1024
GRID_W = 64
NA_HEADS = 8
NA_HEAD_DIM = 64
NA_WIDTH = NA_HEADS * NA_HEAD_DIM
NA_WIN_H = 8
NA_WIN_W = 16
SSD_HEADS = 16
SSD_HEAD_DIM = 64
SSD_WIDTH = SSD_HEADS * SSD_HEAD_DIM
SSD_GROUPS = 2
SSD_HPG = SSD_HEADS // SSD_GROUPS
SSD_GW = SSD_HPG * SSD_HEAD_DIM
SSD_STATE = 128
SSD_CONV = 5
SSD_CHUNK = 128
FNET_GROUPS = 4
FNET_GROUP_DIM = 128
FNET_WIDTH = FNET_GROUPS * FNET_GROUP_DIM
D_MIX = NA_WIDTH + SSD_WIDTH + FNET_WIDTH
RMS_EPS = 1e-6

LANES = 128
VMEM_LIMIT_BYTES = 56 * 1024 * 1024

PROJ_COLS = 4 * NA_WIDTH + SSD_WIDTH + (SSD_WIDTH + 2 * SSD_GROUPS * SSD_STATE) + 2 * FNET_WIDTH
COL_Q, COL_K, COL_V, COL_GNA = 0, 512, 1024, 1536
COL_Z = 2048
COL_X = 3072
COL_B = 4096
COL_C = 4352
COL_F = 4608
COL_GF = 5120
DT_COLS = SSD_GROUPS * LANES

NEG_BIG = -1e30


def _silu(v):
    return v * (1.0 / (1.0 + jnp.exp(-v)))


def _softplus(v):
    return jnp.maximum(v, 0.0) + jnp.log(1.0 + jnp.exp(-jnp.abs(v)))


def _split3(v):
    hi = v.astype(BF16)
    r1 = v - hi.astype(F32)
    mid = r1.astype(BF16)
    lo = (r1 - mid.astype(F32)).astype(BF16)
    return hi, mid, lo


def _dot(a, b):
    return jnp.dot(a, b, preferred_element_type=F32)


def _dot_exact_rhs(a_f32, b_bf16):
    hi, mid, lo = _split3(a_f32)
    return _dot(hi, b_bf16) + _dot(mid, b_bf16) + _dot(lo, b_bf16)


def _dot_exact_lhs(a_bf16, b_f32):
    hi, mid, lo = _split3(b_f32)
    return _dot(a_bf16, hi) + _dot(a_bf16, mid) + _dot(a_bf16, lo)


INPROJ_TM = 1024
INPROJ_TN = PROJ_COLS // 4


def _inproj_body(x_ref, nw_ref, w_ref, wdt_ref, proj_ref, dt_ref, h_scr):
    @pl.when(pl.program_id(1) == 0)
    def _():
        x = x_ref[...]
        ms = jnp.mean(x * x, axis=-1, keepdims=True)
        h = (x * lax.rsqrt(ms + RMS_EPS) * nw_ref[...]).astype(BF16)
        h_scr[...] = h
        dt_ref[...] = _dot(h, wdt_ref[...])

    proj_ref[...] = _dot(h_scr[...], w_ref[...]).astype(BF16)


def _inproj(x2, norm_w, w_main, w_dt):
    m = x2.shape[0]
    tm = min(INPROJ_TM, m)
    return pl.pallas_call(
        _inproj_body,
        grid=(m // tm, PROJ_COLS // INPROJ_TN),
        in_specs=[
            pl.BlockSpec((tm, D_MODEL), lambda i, j: (i, 0)),
            pl.BlockSpec((1, D_MODEL), lambda i, j: (0, 0)),
            pl.BlockSpec((D_MODEL, INPROJ_TN), lambda i, j: (0, j)),
            pl.BlockSpec((D_MODEL, DT_COLS), lambda i, j: (0, 0)),
        ],
        out_specs=[
            pl.BlockSpec((tm, INPROJ_TN), lambda i, j: (i, j)),
            pl.BlockSpec((tm, DT_COLS), lambda i, j: (i, 0)),
        ],
        out_shape=[
            jax.ShapeDtypeStruct((m, PROJ_COLS), BF16),
            jax.ShapeDtypeStruct((m, DT_COLS), F32),
        ],
        scratch_shapes=[pltpu.VMEM((tm, D_MODEL), BF16)],
        compiler_params=pltpu.CompilerParams(
            dimension_semantics=("arbitrary", "arbitrary"),
            vmem_limit_bytes=VMEM_LIMIT_BYTES),
        name="inproj",
    )(x2, norm_w.reshape(1, D_MODEL), w_main, w_dt)


NA_ROWS_PER_STEP = 8
NA_PAIRS = NA_HEADS // 2
NA_BAND = NA_WIN_H * GRID_W


def _natten_body(q_ref, k_ref, v_ref, g_ref, bias_ref, o_ref, *, rows):
    rb = pl.program_id(1)
    lane = lax.broadcasted_iota(jnp.int32, (GRID_W, LANES), 1)
    first_head = lane < NA_HEAD_DIM
    scale = NA_HEAD_DIM ** -0.5

    def one_row(rr, carry):
        r = rb * NA_ROWS_PER_STEP + rr
        r0 = jnp.clip(r - NA_WIN_H // 2, 0, rows - NA_WIN_H)
        delta = r - r0
        kstart = pl.multiple_of(r0 * GRID_W, GRID_W)
        qstart = pl.multiple_of(rr * GRID_W, GRID_W)
        outs = []
        for p in range(NA_PAIRS):
            cols = slice(p * LANES, (p + 1) * LANES)
            q2 = q_ref[0, pl.ds(qstart, GRID_W), cols] * scale
            zero = jnp.zeros_like(q2)
            qs = jnp.concatenate([jnp.where(first_head, q2, zero),
                                  jnp.where(first_head, zero, q2)], axis=0)
            kp = k_ref[0, pl.ds(kstart, NA_BAND), cols]
            vp = v_ref[0, pl.ds(kstart, NA_BAND), cols]
            logits = lax.dot_general(qs, kp, (((1,), (1,)), ((), ())),
                                     preferred_element_type=F32)
            logits = logits + bias_ref[delta, p]
            mx = jnp.max(logits, axis=-1, keepdims=True)
            e = jnp.exp(logits - mx)
            den = jnp.sum(e, axis=-1, keepdims=True)
            o = _dot(e.astype(BF16), vp) / den
            outs.append(jnp.where(first_head, o[:GRID_W], o[GRID_W:]))
        att = jnp.concatenate(outs, axis=1)
        gate = g_ref[0, pl.ds(qstart, GRID_W), :].astype(F32)
        o_ref[0, pl.ds(qstart, GRID_W), :] = (att * _silu(gate)).astype(BF16)
        return carry

    lax.fori_loop(0, NA_ROWS_PER_STEP, one_row, 0)


def _natten(proj3, bias_tab):
    b, s, _ = proj3.shape
    rows = s // GRID_W
    tq = NA_ROWS_PER_STEP * GRID_W
    blk = NA_WIDTH // 512
    return pl.pallas_call(
        functools.partial(_natten_body, rows=rows),
        grid=(b, rows // NA_ROWS_PER_STEP),
        in_specs=[
            pl.BlockSpec((1, tq, NA_WIDTH), lambda i, j: (i, j, COL_Q // NA_WIDTH)),
            pl.BlockSpec((1, s, NA_WIDTH), lambda i, j: (i, 0, COL_K // NA_WIDTH)),
            pl.BlockSpec((1, s, NA_WIDTH), lambda i, j: (i, 0, COL_V // NA_WIDTH)),
            pl.BlockSpec((1, tq, NA_WIDTH), lambda i, j: (i, j, COL_GNA // NA_WIDTH)),
            pl.BlockSpec((NA_WIN_H, NA_PAIRS, 2 * GRID_W, NA_BAND), lambda i, j: (0, 0, 0, 0)),
        ],
        out_specs=pl.BlockSpec((1, tq, NA_WIDTH), lambda i, j: (i, j, 0)),
        out_shape=jax.ShapeDtypeStruct((b, s, NA_WIDTH), BF16),
        compiler_params=pltpu.CompilerParams(
            dimension_semantics=("arbitrary", "arbitrary"),
            vmem_limit_bytes=VMEM_LIMIT_BYTES),
        name="natten",
    )(proj3, proj3, proj3, proj3, bias_tab)


def _natten_bias_table(rpb):
    delta = np.arange(NA_WIN_H)[:, None, None, None]
    w = np.arange(NA_WIN_H)[None, :, None, None]
    qc = np.arange(GRID_W)[None, None, :, None]
    kc = np.arange(GRID_W)[None, None, None, :]
    row_rel = np.broadcast_to(w - delta + (NA_WIN_H - 1), (NA_WIN_H, NA_WIN_H, GRID_W, GRID_W))
    col_start = np.clip(qc - NA_WIN_W // 2, 0, GRID_W - NA_WIN_W)
    inside = np.broadcast_to((kc >= col_start) & (kc < col_start + NA_WIN_W),
                             (NA_WIN_H, NA_WIN_H, GRID_W, GRID_W))
    col_rel = np.broadcast_to(np.clip(kc - qc + (NA_WIN_W - 1), 0, 2 * NA_WIN_W - 2),
                              (NA_WIN_H, NA_WIN_H, GRID_W, GRID_W))
    tab = rpb.astype(F32)[:, row_rel, col_rel]
    tab = jnp.where(inside[None], tab, NEG_BIG)
    tab = tab.transpose(1, 0, 3, 2, 4)
    return tab.reshape(NA_WIN_H, NA_PAIRS, 2 * GRID_W, NA_BAND)


CONV_HALO = 16


def _conv_silu(u_ref, col0, width, w, bias, c, nchunks):
    L = SSD_CHUNK
    start = pl.multiple_of(c * L, L)
    prev_start = pl.multiple_of(jnp.maximum(c * L - CONV_HALO, 0), CONV_HALO)
    next_start = pl.multiple_of(jnp.minimum(c * L + L, (nchunks - 1) * L + L - CONV_HALO), CONV_HALO)
    cols = slice(col0, col0 + width)
    cur = u_ref[0, pl.ds(start, L), cols].astype(F32)
    prev = u_ref[0, pl.ds(prev_start, CONV_HALO), cols].astype(F32)
    nxt = u_ref[0, pl.ds(next_start, CONV_HALO), cols].astype(F32)
    prev = jnp.where(c > 0, prev, 0.0)
    nxt = jnp.where(c < nchunks - 1, nxt, 0.0)
    u = jnp.concatenate([prev, cur, nxt], axis=0)
    n = L + 2 * CONV_HALO
    pad = SSD_CONV // 2
    acc = bias
    for j in range(SSD_CONV):
        shift = (pad - j) % n
        shifted = u if shift == 0 else pltpu.roll(u, shift, 0)
        acc = acc + shifted[CONV_HALO:CONV_HALO + L] * w[j:j + 1, :]
    return _silu(acc)


def _ssd_body(x_ref, b_ref, c_ref, z_ref, dt_ref, cwx_ref, cwbc_ref, cbx_ref, cbbc_ref,
              dtb_ref, alog_ref, dskip_ref, nw_ref, tri_ref, exp_ref,
              o_ref, xc_scr, bc_scr, cc_scr, y_scr, st_scr, *, nchunks):
    L = SSD_CHUNK
    N = SSD_STATE

    def prologue(c, carry):
        rows = pl.ds(pl.multiple_of(c * L, L), L)
        xv = _conv_silu(x_ref, 0, SSD_GW, cwx_ref[0], cbx_ref[0], c, nchunks)
        xc_scr[rows, :] = xv.astype(BF16)
        y_scr[rows, :] = xv * dskip_ref[0]
        bv = _conv_silu(b_ref, 0, N, cwbc_ref[0, :, 0:N], cbbc_ref[0, :, 0:N], c, nchunks)
        bc_scr[rows, :] = bv.astype(BF16)
        cv = _conv_silu(c_ref, 0, N, cwbc_ref[0, :, N:2 * N], cbbc_ref[0, :, N:2 * N], c, nchunks)
        cc_scr[rows, :] = cv.astype(BF16)
        return carry

    lax.fori_loop(0, nchunks, prologue, 0)
    st_scr[...] = jnp.zeros_like(st_scr)

    a_row = -jnp.exp(alog_ref[0])
    dtb_row = dtb_ref[0]
    tri = tri_ref[...]
    li = lax.broadcasted_iota(jnp.int32, (L, L), 0)
    si = lax.broadcasted_iota(jnp.int32, (L, L), 1)
    lane = lax.broadcasted_iota(jnp.int32, (L, LANES), 1)
    first_head = lane < SSD_HEAD_DIM

    def chunk_step(c, d):
        rows = pl.ds(pl.multiple_of(c * L, L), L)
        dt = _softplus(dt_ref[0, rows, :] + dtb_row)
        adt = dt * a_row
        cs = _dot_exact_lhs(tri, adt)
        expand = exp_ref[d]
        if d == 0:
            u = cs
            mask = li >= si
            u_e = _dot_exact_rhs(cs, expand)
            last_e = u_e[L - 1:L, :]
            out_scale = jnp.exp(u_e)
            st_w = jnp.exp(last_e - u_e)
            chunk_decay = jnp.exp(last_e)
        else:
            u = adt - cs
            mask = si >= li
            u_e = _dot_exact_rhs(u, expand)
            tot_e = _dot_exact_rhs(cs[L - 8:L, :], expand)[7:8, :]
            out_scale = jnp.exp(tot_e + u_e)
            st_w = jnp.exp(-u_e)
            chunk_decay = jnp.exp(tot_e)
        u_t = u.T
        dt_e = _dot_exact_rhs(dt, expand)
        xcf = xc_scr[rows, :].astype(F32)
        xdf = xcf * dt_e
        xd = xdf.astype(BF16)
        bcv = bc_scr[rows, :]
        ccv = cc_scr[rows, :]
        g = lax.dot_general(ccv, bcv, (((1,), (1,)), ((), ())), preferred_element_type=F32)
        parts = []
        for p in range(SSD_HPG // 2):
            xd_pair = xd[:, p * LANES:(p + 1) * LANES]
            ys = []
            for hh in range(2):
                col = d * SSD_HPG + 2 * p + hh
                diff = u[:, col:col + 1] - u_t[col:col + 1, :]
                dec = jnp.exp(jnp.where(mask, diff, NEG_BIG))
                ys.append(_dot((g * dec).astype(BF16), xd_pair))
            parts.append(jnp.where(first_head, ys[0], ys[1]))
        y_diag = jnp.concatenate(parts, axis=1)
        st = st_scr[d]
        y_off = _dot(ccv, st.astype(BF16)) * out_scale
        y_scr[rows, :] += y_diag + y_off
        wts = (xdf * st_w).astype(BF16)
        st_scr[d] = st * chunk_decay + lax.dot_general(
            bcv, wts, (((0,), (0,)), ((), ())), preferred_element_type=F32)

    def main(i, carry):
        chunk_step(i, 0)
        chunk_step(nchunks - 1 - i, 1)
        return carry

    lax.fori_loop(0, nchunks, main, 0)

    def epilogue(c, carry):
        rows = pl.ds(pl.multiple_of(c * L, L), L)
        y = y_scr[rows, :] * _silu(z_ref[0, rows, :].astype(F32))
        ms = jnp.mean(y * y, axis=-1, keepdims=True)
        o_ref[0, rows, :] = (y * lax.rsqrt(ms + RMS_EPS) * nw_ref[0]).astype(BF16)
        return carry

    lax.fori_loop(0, nchunks, epilogue, 0)


def _ssd(proj3, dt3, cw_x, cw_bc, cb_x, cb_bc, dtb, alog, dskip, nw, tri, expand):
    b, s, _ = proj3.shape
    nchunks = s // SSD_CHUNK
    N = SSD_STATE
    seq = lambda width, col: pl.BlockSpec((1, s, width), lambda i, g, col=col, width=width: (i, 0, col // width + g))
    per_group = lambda shape: pl.BlockSpec((1,) + shape, lambda i, g: (g,) + (0,) * len(shape))
    return pl.pallas_call(
        functools.partial(_ssd_body, nchunks=nchunks),
        grid=(b, SSD_GROUPS),
        in_specs=[
            seq(SSD_GW, COL_X), seq(N, COL_B), seq(N, COL_C), seq(SSD_GW, COL_Z),
            pl.BlockSpec((1, s, LANES), lambda i, g: (i, 0, g)),
            per_group((SSD_CONV, SSD_GW)), per_group((SSD_CONV, 2 * N)),
            per_group((1, SSD_GW)), per_group((1, 2 * N)),
            per_group((1, LANES)), per_group((1, LANES)),
            per_group((1, SSD_GW)), per_group((1, SSD_GW)),
            pl.BlockSpec((SSD_CHUNK, SSD_CHUNK), lambda i, g: (0, 0)),
            pl.BlockSpec((2, LANES, SSD_GW), lambda i, g: (0, 0, 0)),
        ],
        out_specs=pl.BlockSpec((1, s, SSD_GW), lambda i, g: (i, 0, g)),
        out_shape=jax.ShapeDtypeStruct((b, s, SSD_WIDTH), BF16),
        scratch_shapes=[
            pltpu.VMEM((s, SSD_GW), BF16),
            pltpu.VMEM((s, N), BF16),
            pltpu.VMEM((s, N), BF16),
            pltpu.VMEM((s, SSD_GW), F32),
            pltpu.VMEM((2, N, SSD_GW), F32),
        ],
        compiler_params=pltpu.CompilerParams(
            dimension_semantics=("arbitrary", "arbitrary"),
            vmem_limit_bytes=VMEM_LIMIT_BYTES),
        name="ssd",
    )(proj3, proj3, proj3, proj3, dt3, cw_x, cw_bc, cb_x, cb_bc, dtb, alog, dskip, nw, tri, expand)


def _ssd_constants():
    tri = np.tril(np.ones((SSD_CHUNK, SSD_CHUNK), np.float32))
    expand = np.zeros((2, LANES, SSD_GW), np.float32)
    for d in range(2):
        for h in range(SSD_HPG):
            expand[d, d * SSD_HPG + h, h * SSD_HEAD_DIM:(h + 1) * SSD_HEAD_DIM] = 1.0
    return jnp.asarray(tri, BF16), jnp.asarray(expand, BF16)


FNET_CW = 256
FNET_N2 = GRID_W


def _fnet_body(u_ref, g_ref, wf_ref, cs_ref, r1_ref, e2_ref, o_ref, re_scr, im_scr, out_scr, *, n1):
    n2 = FNET_N2
    s = n1 * n2
    gd = FNET_GROUP_DIM
    ngrp = FNET_CW // gd
    tr = 512 if s % 512 == 0 else s

    def chan(t, carry):
        rows = pl.ds(pl.multiple_of(t * tr, tr), tr)
        for gi in range(ngrp):
            ri = _dot(u_ref[0, rows, gi * gd:(gi + 1) * gd], cs_ref[...])
            re_scr[gi, rows, :] = ri[:, :gd]
            im_scr[gi, rows, :] = ri[:, gd:]
        return carry

    lax.fori_loop(0, s // tr, chan, 0)

    def gather(scr, idx):
        return jnp.concatenate([scr[gi, idx, :] for gi in range(ngrp)], axis=1)

    def scatter(scr, idx, val):
        for gi in range(ngrp):
            scr[gi, idx, :] = val[:, gi * gd:(gi + 1) * gd]

    def stage1(c, carry):
        idx = pl.ds(c, n1, stride=n2)
        rhs = jnp.concatenate([gather(re_scr, idx), gather(im_scr, idx)], axis=0).astype(BF16)
        t = _dot(r1_ref[...], rhs)
        scatter(re_scr, idx, t[:n1])
        scatter(im_scr, idx, t[n1:])
        return carry

    lax.fori_loop(0, n2, stage1, 0)

    def stage2(k1, carry):
        blk = pl.ds(pl.multiple_of(k1 * n2, n2), n2)
        rhs = jnp.concatenate([gather(re_scr, blk), gather(im_scr, blk)], axis=0).astype(BF16)
        scatter(out_scr, pl.ds(k1, n2, stride=n1), _dot(e2_ref[blk, :], rhs))
        return carry

    lax.fori_loop(0, n1, stage2, 0)

    ortho = 1.0 / math.sqrt(s * gd)

    def mix(t, carry):
        rows = pl.ds(pl.multiple_of(t * tr, tr), tr)
        outs = [_dot(out_scr[gi, rows, :].astype(BF16), wf_ref[gi]) for gi in range(ngrp)]
        y = jnp.concatenate(outs, axis=1) * ortho
        o_ref[0, rows, :] = (y * _silu(g_ref[0, rows, :].astype(F32))).astype(BF16)
        return carry

    lax.fori_loop(0, s // tr, mix, 0)


def _fnet(proj3, w_f, cs_ch, r1, e2):
    b, s, _ = proj3.shape
    n1 = s // FNET_N2
    nblk = FNET_WIDTH // FNET_CW
    ngrp = FNET_CW // FNET_GROUP_DIM
    return pl.pallas_call(
        functools.partial(_fnet_body, n1=n1),
        grid=(b, nblk),
        in_specs=[
            pl.BlockSpec((1, s, FNET_CW), lambda i, j: (i, 0, COL_F // FNET_CW + j)),
            pl.BlockSpec((1, s, FNET_CW), lambda i, j: (i, 0, COL_GF // FNET_CW + j)),
            pl.BlockSpec((ngrp, FNET_GROUP_DIM, FNET_GROUP_DIM), lambda i, j: (j, 0, 0)),
            pl.BlockSpec((FNET_GROUP_DIM, 2 * FNET_GROUP_DIM), lambda i, j: (0, 0)),
            pl.BlockSpec((2 * n1, 2 * n1), lambda i, j: (0, 0)),
            pl.BlockSpec((s, 2 * FNET_N2), lambda i, j: (0, 0)),
        ],
        out_specs=pl.BlockSpec((1, s, FNET_CW), lambda i, j: (i, 0, j)),
        out_shape=jax.ShapeDtypeStruct((b, s, FNET_WIDTH), BF16),
        scratch_shapes=[pltpu.VMEM((ngrp, s, FNET_GROUP_DIM), F32)] * 3,
        compiler_params=pltpu.CompilerParams(
            dimension_semantics=("arbitrary", "arbitrary"),
            vmem_limit_bytes=VMEM_LIMIT_BYTES),
        name="fnet",
    )(proj3, proj3, w_f, cs_ch, r1, e2)


def _fnet_constants(s):
    n2 = FNET_N2
    n1 = s // n2
    gd = FNET_GROUP_DIM
    jk = np.outer(np.arange(gd), np.arange(gd)) * (2.0 * np.pi / gd)
    cs_ch = np.concatenate([np.cos(jk), -np.sin(jk)], axis=1)
    rk = np.outer(np.arange(n1), np.arange(n1)) * (2.0 * np.pi / n1)
    cr, sr = np.cos(rk), np.sin(rk)
    r1 = np.block([[cr, sr], [-sr, cr]])
    k1 = np.arange(n1)[:, None, None]
    k2 = np.arange(n2)[None, :, None]
    c = np.arange(n2)[None, None, :]
    theta = (2.0 * np.pi / s) * ((c * (k1 + n1 * k2)) % s)
    e2 = np.concatenate([np.cos(theta), np.sin(theta)], axis=2).reshape(n1 * n2, 2 * n2)
    return (jnp.asarray(cs_ch, BF16), jnp.asarray(r1, BF16), jnp.asarray(e2, BF16))


OUTPROJ_TM = 1024


def _outproj_body(x_ref, na_ref, ssd_ref, fn_ref, w_ref, fw_ref, o_ref, *, final):
    acc = _dot(na_ref[...], w_ref[0:NA_WIDTH, :])
    acc = acc + _dot(ssd_ref[...], w_ref[NA_WIDTH:NA_WIDTH + SSD_WIDTH, :])
    acc = acc + _dot(fn_ref[...], w_ref[NA_WIDTH + SSD_WIDTH:D_MIX, :])
    y = x_ref[...] + acc
    if final:
        ms = jnp.mean(y * y, axis=-1, keepdims=True)
        y = y * lax.rsqrt(ms + RMS_EPS) * fw_ref[...]
    o_ref[...] = y


def _outproj(x2, na2, ssd2, fn2, w_out, final_w, final):
    m = x2.shape[0]
    tm = min(OUTPROJ_TM, m)
    row = lambda width: pl.BlockSpec((tm, width), lambda i: (i, 0))
    return pl.pallas_call(
        functools.partial(_outproj_body, final=final),
        grid=(m // tm,),
        in_specs=[
            row(D_MODEL), row(NA_WIDTH), row(SSD_WIDTH), row(FNET_WIDTH),
            pl.BlockSpec((D_MIX, D_MODEL), lambda i: (0, 0)),
            pl.BlockSpec((1, D_MODEL), lambda i: (0, 0)),
        ],
        out_specs=row(D_MODEL),
        out_shape=jax.ShapeDtypeStruct((m, D_MODEL), F32),
        compiler_params=pltpu.CompilerParams(
            dimension_semantics=("arbitrary",),
            vmem_limit_bytes=VMEM_LIMIT_BYTES),
        name="outproj_final" if final else "outproj",
    )(x2, na2, ssd2, fn2, w_out, final_w.reshape(1, D_MODEL))


def _layer_params(w_in, conv_w, conv_b, dt_bias, a_log, d_skip, ssd_norm_w):
    dt0 = 4 * NA_WIDTH + SSD_WIDTH + (SSD_WIDTH + 2 * SSD_GROUPS * SSD_STATE)
    w_main = jnp.concatenate([w_in[:, :dt0], w_in[:, dt0 + 2 * SSD_HEADS:]], axis=1).astype(BF16)
    wdt = w_in[:, dt0:dt0 + 2 * SSD_HEADS].reshape(D_MODEL, 2, SSD_GROUPS, SSD_HPG)
    wdt = wdt.transpose(0, 2, 1, 3).reshape(D_MODEL, SSD_GROUPS, 2 * SSD_HPG)
    wdt = jnp.pad(wdt, ((0, 0), (0, 0), (0, LANES - 2 * SSD_HPG))).reshape(D_MODEL, DT_COLS).astype(BF16)

    def head_rows(v):
        v = v.reshape(2, SSD_GROUPS, SSD_HPG).transpose(1, 0, 2).reshape(SSD_GROUPS, 1, 2 * SSD_HPG)
        return jnp.pad(v, ((0, 0), (0, 0), (0, LANES - 2 * SSD_HPG)))

    N = SSD_STATE
    cw_x = conv_w[:, :SSD_WIDTH].reshape(SSD_CONV, SSD_GROUPS, SSD_GW).transpose(1, 0, 2)
    cb_x = conv_b[:SSD_WIDTH].reshape(SSD_GROUPS, 1, SSD_GW)
    cw_b = conv_w[:, SSD_WIDTH:SSD_WIDTH + SSD_GROUPS * N].reshape(SSD_CONV, SSD_GROUPS, N)
    cw_c = conv_w[:, SSD_WIDTH + SSD_GROUPS * N:].reshape(SSD_CONV, SSD_GROUPS, N)
    cw_bc = jnp.concatenate([cw_b, cw_c], axis=2).transpose(1, 0, 2)
    cb_b = conv_b[SSD_WIDTH:SSD_WIDTH + SSD_GROUPS * N].reshape(SSD_GROUPS, 1, N)
    cb_c = conv_b[SSD_WIDTH + SSD_GROUPS * N:].reshape(SSD_GROUPS, 1, N)
    cb_bc = jnp.concatenate([cb_b, cb_c], axis=2)
    dskip = jnp.repeat(d_skip, SSD_HEAD_DIM).reshape(SSD_GROUPS, 1, SSD_GW)
    nw = ssd_norm_w.reshape(SSD_GROUPS, 1, SSD_GW)
    return dict(w_main=w_main, w_dt=wdt, cw_x=cw_x, cw_bc=cw_bc, cb_x=cb_x, cb_bc=cb_bc,
                dtb=head_rows(dt_bias), alog=head_rows(a_log), dskip=dskip, nw=nw)


def kernel(x, norm_w, w_in, na_rpb, conv_w, conv_b, dt_bias, a_log, d_skip, ssd_norm_w,
           w_fourier, w_out, final_norm_w):
    b, s, d = x.shape
    depth = w_in.shape[0]
    tri, expand = _ssd_constants()
    cs_ch, r1, e2 = _fnet_constants(s)
    x2 = x.reshape(b * s, d)
    for i in range(depth):
        p = _layer_params(w_in[i], conv_w[i], conv_b[i], dt_bias[i], a_log[i], d_skip[i], ssd_norm_w[i])
        proj, dt = _inproj(x2, norm_w[i], p["w_main"], p["w_dt"])
        proj3 = proj.reshape(b, s, PROJ_COLS)
        dt3 = dt.reshape(b, s, DT_COLS)
        na = _natten(proj3, _natten_bias_table(na_rpb[i]))
        ssd = _ssd(proj3, dt3, p["cw_x"], p["cw_bc"], p["cb_x"], p["cb_bc"], p["dtb"], p["alog"],
                   p["dskip"], p["nw"], tri, expand)
        fn = _fnet(proj3, w_fourier[i].astype(BF16), cs_ch, r1, e2)
        x2 = _outproj(x2, na.reshape(b * s, NA_WIDTH), ssd.reshape(b * s, SSD_WIDTH),
                      fn.reshape(b * s, FNET_WIDTH), w_out[i].astype(BF16), final_norm_w,
                      final=(i == depth - 1))
    return x2.reshape(b, s, d)
```

```python
import functools
import math

import numpy as np
import jax
import jax.numpy as jnp
from jax import lax
from jax.experimental import pallas as pl
from jax.experimental.pallas import tpu as pltpu

F32 = jnp.float32
BF16 = jnp.bfloat16

D_MODEL = 1024
GRID_W = 64
NA_HEADS = 8
NA_HEAD_DIM = 64
NA_WIDTH = NA_HEADS * NA_HEAD_DIM
NA_WIN_H = 8
NA_WIN_W = 16
SSD_HEADS = 16
SSD_HEAD_DIM = 64
SSD_WIDTH = SSD_HEADS * SSD_HEAD_DIM
SSD_GROUPS = 2
SSD_HPG = SSD_HEADS // SSD_GROUPS
SSD_GW = SSD_HPG * SSD_HEAD_DIM
SSD_STATE = 128
SSD_CONV = 5
SSD_CHUNK = 128
FNET_GROUPS = 4
FNET_GROUP_DIM = 128
FNET_WIDTH = FNET_GROUPS * FNET_GROUP_DIM
D_MIX = NA_WIDTH + SSD_WIDTH + FNET_WIDTH
RMS_EPS = 1e-6

LANES = 128
VMEM_LIMIT_BYTES = 56 * 1024 * 1024

PROJ_COLS = 4 * NA_WIDTH + SSD_WIDTH + (SSD_WIDTH + 2 * SSD_GROUPS * SSD_STATE) + 2 * FNET_WIDTH
COL_Q, COL_K, COL_V, COL_GNA = 0, 512, 1024, 1536
COL_Z = 2048
COL_X = 3072
COL_B = 4096
COL_C = 4352
COL_F = 4608
COL_GF = 5120
DT_COLS = SSD_GROUPS * LANES

NEG_BIG = -1e30


def _silu(v):
    return v * (1.0 / (1.0 + jnp.exp(-v)))


def _softplus(v):
    return jnp.maximum(v, 0.0) + jnp.log(1.0 + jnp.exp(-jnp.abs(v)))


def _split3(v):
    hi = v.astype(BF16)
    r1 = v - hi.astype(F32)
    mid = r1.astype(BF16)
    lo = (r1 - mid.astype(F32)).astype(BF16)
    return hi, mid, lo


def _dot(a, b):
    return jnp.dot(a, b, preferred_element_type=F32)


def _dot_exact_rhs(a_f32, b_bf16):
    hi, mid, lo = _split3(a_f32)
    return _dot(hi, b_bf16) + _dot(mid, b_bf16) + _dot(lo, b_bf16)


def _dot_exact_lhs(a_bf16, b_f32):
    hi, mid, lo = _split3(b_f32)
    return _dot(a_bf16, hi) + _dot(a_bf16, mid) + _dot(a_bf16, lo)


INPROJ_TM = 1024
INPROJ_TN = PROJ_COLS // 4


def _inproj_body(x_ref, nw_ref, w_ref, wdt_ref, proj_ref, dt_ref, h_scr):
    @pl.when(pl.program_id(1) == 0)
    def _():
        x = x_ref[...]
        ms = jnp.mean(x * x, axis=-1, keepdims=True)
        h = (x * lax.rsqrt(ms + RMS_EPS) * nw_ref[...]).astype(BF16)
        h_scr[...] = h
        dt_ref[...] = _dot(h, wdt_ref[...])

    proj_ref[...] = _dot(h_scr[...], w_ref[...]).astype(BF16)


def _inproj(x2, norm_w, w_main, w_dt):
    m = x2.shape[0]
    tm = min(INPROJ_TM, m)
    return pl.pallas_call(
        _inproj_body,
        grid=(m // tm, PROJ_COLS // INPROJ_TN),
        in_specs=[
            pl.BlockSpec((tm, D_MODEL), lambda i, j: (i, 0)),
            pl.BlockSpec((1, D_MODEL), lambda i, j: (0, 0)),
            pl.BlockSpec((D_MODEL, INPROJ_TN), lambda i, j: (0, j)),
            pl.BlockSpec((D_MODEL, DT_COLS), lambda i, j: (0, 0)),
        ],
        out_specs=[
            pl.BlockSpec((tm, INPROJ_TN), lambda i, j: (i, j)),
            pl.BlockSpec((tm, DT_COLS), lambda i, j: (i, 0)),
        ],
        out_shape=[
            jax.ShapeDtypeStruct((m, PROJ_COLS), BF16),
            jax.ShapeDtypeStruct((m, DT_COLS), F32),
        ],
        scratch_shapes=[pltpu.VMEM((tm, D_MODEL), BF16)],
        compiler_params=pltpu.CompilerParams(
            dimension_semantics=("arbitrary", "arbitrary"),
            vmem_limit_bytes=VMEM_LIMIT_BYTES),
        name="inproj",
    )(x2, norm_w.reshape(1, D_MODEL), w_main, w_dt)


NA_ROWS_PER_STEP = 8
NA_PAIRS = NA_HEADS // 2
NA_BAND = NA_WIN_H * GRID_W


def _natten_body(q_ref, k_ref, v_ref, g_ref, bias_ref, o_ref, *, rows):
    rb = pl.program_id(1)
    lane = lax.broadcasted_iota(jnp.int32, (GRID_W, LANES), 1)
    first_head = lane < NA_HEAD_DIM
    scale = NA_HEAD_DIM ** -0.5

    def one_row(rr, carry):
        r = rb * NA_ROWS_PER_STEP + rr
        r0 = jnp.clip(r - NA_WIN_H // 2, 0, rows - NA_WIN_H)
        delta = r - r0
        kstart = pl.multiple_of(r0 * GRID_W, GRID_W)
        qstart = pl.multiple_of(rr * GRID_W, GRID_W)
        outs = []
        for p in range(NA_PAIRS):
            cols = slice(p * LANES, (p + 1) * LANES)
            q2 = q_ref[0, pl.ds(qstart, GRID_W), cols] * scale
            zero = jnp.zeros_like(q2)
            qs = jnp.concatenate([jnp.where(first_head, q2, zero),
                                  jnp.where(first_head, zero, q2)], axis=0)
            kp = k_ref[0, pl.ds(kstart, NA_BAND), cols]
            vp = v_ref[0, pl.ds(kstart, NA_BAND), cols]
            logits = lax.dot_general(qs, kp, (((1,), (1,)), ((), ())),
                                     preferred_element_type=F32)
            logits = logits + bias_ref[delta, p]
            mx = jnp.max(logits, axis=-1, keepdims=True)
            e = jnp.exp(logits - mx)
            den = jnp.sum(e, axis=-1, keepdims=True)
            o = _dot(e.astype(BF16), vp) / den
            outs.append(jnp.where(first_head, o[:GRID_W], o[GRID_W:]))
        att = jnp.concatenate(outs, axis=1)
        gate = g_ref[0, pl.ds(qstart, GRID_W), :].astype(F32)
        o_ref[0, pl.ds(qstart, GRID_W), :] = (att * _silu(gate)).astype(BF16)
        return carry

    lax.fori_loop(0, NA_ROWS_PER_STEP, one_row, 0)


def _natten(proj3, bias_tab):
    b, s, _ = proj3.shape
    rows = s // GRID_W
    tq = NA_ROWS_PER_STEP * GRID_W
    blk = NA_WIDTH // 512
    return pl.pallas_call(
        functools.partial(_natten_body, rows=rows),
        grid=(b, rows // NA_ROWS_PER_STEP),
        in_specs=[
            pl.BlockSpec((1, tq, NA_WIDTH), lambda i, j: (i, j, COL_Q // NA_WIDTH)),
            pl.BlockSpec((1, s, NA_WIDTH), lambda i, j: (i, 0, COL_K // NA_WIDTH)),
            pl.BlockSpec((1, s, NA_WIDTH), lambda i, j: (i, 0, COL_V // NA_WIDTH)),
            pl.BlockSpec((1, tq, NA_WIDTH), lambda i, j: (i, j, COL_GNA // NA_WIDTH)),
            pl.BlockSpec((NA_WIN_H, NA_PAIRS, 2 * GRID_W, NA_BAND), lambda i, j: (0, 0, 0, 0)),
        ],
        out_specs=pl.BlockSpec((1, tq, NA_WIDTH), lambda i, j: (i, j, 0)),
        out_shape=jax.ShapeDtypeStruct((b, s, NA_WIDTH), BF16),
        compiler_params=pltpu.CompilerParams(
            dimension_semantics=("arbitrary", "arbitrary"),
            vmem_limit_bytes=VMEM_LIMIT_BYTES),
        name="natten",
    )(proj3, proj3, proj3, proj3, bias_tab)


def _natten_bias_table(rpb):
    qc = np.arange(GRID_W)[:, None]
    kc = np.arange(GRID_W)[None, :]
    col_start = np.clip(qc - NA_WIN_W // 2, 0, GRID_W - NA_WIN_W)
    inside = (kc >= col_start) & (kc < col_start + NA_WIN_W)
    col_rel = np.clip(kc - qc + (NA_WIN_W - 1), 0, 2 * NA_WIN_W - 2)
    onehot = (np.arange(2 * NA_WIN_W - 1)[None, None, :] == col_rel[:, :, None]) & inside[:, :, None]
    toe = jnp.einsum("hrj,qkj->hrqk", rpb.astype(F32), jnp.asarray(onehot, F32),
                     precision=lax.Precision.HIGHEST)
    toe = toe + jnp.asarray(np.where(inside, 0.0, NEG_BIG), F32)
    tab = jnp.stack([toe[:, NA_WIN_H - 1 - dl:2 * NA_WIN_H - 1 - dl] for dl in range(NA_WIN_H)])
    tab = tab.transpose(0, 1, 3, 2, 4)
    return tab.reshape(NA_WIN_H, NA_PAIRS, 2 * GRID_W, NA_BAND)


CONV_HALO = 16


def _conv_silu(u_ref, col0, width, w, bias, c, nchunks):
    L = SSD_CHUNK
    start = pl.multiple_of(c * L, L)
    prev_start = pl.multiple_of(jnp.maximum(c * L - CONV_HALO, 0), CONV_HALO)
    next_start = pl.multiple_of(jnp.minimum(c * L + L, (nchunks - 1) * L + L - CONV_HALO), CONV_HALO)
    cols = slice(col0, col0 + width)
    cur = u_ref[0, pl.ds(start, L), cols].astype(F32)
    prev = u_ref[0, pl.ds(prev_start, CONV_HALO), cols].astype(F32)
    nxt = u_ref[0, pl.ds(next_start, CONV_HALO), cols].astype(F32)
    prev = jnp.where(c > 0, prev, 0.0)
    nxt = jnp.where(c < nchunks - 1, nxt, 0.0)
    u = jnp.concatenate([prev, cur, nxt], axis=0)
    n = L + 2 * CONV_HALO
    pad = SSD_CONV // 2
    acc = bias
    for j in range(SSD_CONV):
        shift = (pad - j) % n
        shifted = u if shift == 0 else pltpu.roll(u, shift, 0)
        acc = acc + shifted[CONV_HALO:CONV_HALO + L] * w[j:j + 1, :]
    return _silu(acc)


def _ssd_body(x_ref, b_ref, c_ref, z_ref, dt_ref, cwx_ref, cwbc_ref, cbx_ref, cbbc_ref,
              dtb_ref, alog_ref, dskip_ref, nw_ref, tri_ref, exp_ref,
              o_ref, xc_scr, bc_scr, cc_scr, y_scr, st_scr, *, nchunks):
    L = SSD_CHUNK
    N = SSD_STATE

    def prologue(c, carry):
        rows = pl.ds(pl.multiple_of(c * L, L), L)
        xv = _conv_silu(x_ref, 0, SSD_GW, cwx_ref[0], cbx_ref[0], c, nchunks)
        xc_scr[rows, :] = xv.astype(BF16)
        y_scr[rows, :] = xv * dskip_ref[0]
        bv = _conv_silu(b_ref, 0, N, cwbc_ref[0, :, 0:N], cbbc_ref[0, :, 0:N], c, nchunks)
        bc_scr[rows, :] = bv.astype(BF16)
        cv = _conv_silu(c_ref, 0, N, cwbc_ref[0, :, N:2 * N], cbbc_ref[0, :, N:2 * N], c, nchunks)
        cc_scr[rows, :] = cv.astype(BF16)
        return carry

    lax.fori_loop(0, nchunks, prologue, 0)
    st_scr[...] = jnp.zeros_like(st_scr)

    a_row = -jnp.exp(alog_ref[0])
    dtb_row = dtb_ref[0]
    tri = tri_ref[...]
    li = lax.broadcasted_iota(jnp.int32, (L, L), 0)
    si = lax.broadcasted_iota(jnp.int32, (L, L), 1)
    lane = lax.broadcasted_iota(jnp.int32, (L, LANES), 1)
    first_head = lane < SSD_HEAD_DIM

    def chunk_step(c, d):
        rows = pl.ds(pl.multiple_of(c * L, L), L)
        dt = _softplus(dt_ref[0, rows, :] + dtb_row)
        adt = dt * a_row
        cs = _dot_exact_lhs(tri, adt)
        expand = exp_ref[d]
        if d == 0:
            u = cs
            mask = li >= si
            u_e = _dot_exact_rhs(cs, expand)
            last_e = u_e[L - 1:L, :]
            out_scale = jnp.exp(u_e)
            st_w = jnp.exp(last_e - u_e)
            chunk_decay = jnp.exp(last_e)
        else:
            u = adt - cs
            mask = si >= li
            u_e = _dot_exact_rhs(u, expand)
            tot_e = _dot_exact_rhs(cs[L - 8:L, :], expand)[7:8, :]
            out_scale = jnp.exp(tot_e + u_e)
            st_w = jnp.exp(-u_e)
            chunk_decay = jnp.exp(tot_e)
        u_t = u.T
        dt_e = _dot_exact_rhs(dt, expand)
        xcf = xc_scr[rows, :].astype(F32)
        xdf = xcf * dt_e
        xd = xdf.astype(BF16)
        bcv = bc_scr[rows, :]
        ccv = cc_scr[rows, :]
        g = lax.dot_general(ccv, bcv, (((1,), (1,)), ((), ())), preferred_element_type=F32)
        parts = []
        for p in range(SSD_HPG // 2):
            xd_pair = xd[:, p * LANES:(p + 1) * LANES]
            ys = []
            for hh in range(2):
                col = d * SSD_HPG + 2 * p + hh
                diff = u[:, col:col + 1] - u_t[col:col + 1, :]
                dec = jnp.exp(jnp.where(mask, diff, NEG_BIG))
                ys.append(_dot((g * dec).astype(BF16), xd_pair))
            parts.append(jnp.where(first_head, ys[0], ys[1]))
        y_diag = jnp.concatenate(parts, axis=1)
        st = st_scr[d]
        y_off = _dot(ccv, st.astype(BF16)) * out_scale
        y_scr[rows, :] += y_diag + y_off
        wts = (xdf * st_w).astype(BF16)
        st_scr[d] = st * chunk_decay + lax.dot_general(
            bcv, wts, (((0,), (0,)), ((), ())), preferred_element_type=F32)

    def main(i, carry):
        chunk_step(i, 0)
        chunk_step(nchunks - 1 - i, 1)
        return carry

    lax.fori_loop(0, nchunks, main, 0)

    def epilogue(c, carry):
        rows = pl.ds(pl.multiple_of(c * L, L), L)
        y = y_scr[rows, :] * _silu(z_ref[0, rows, :].astype(F32))
        ms = jnp.mean(y * y, axis=-1, keepdims=True)
        o_ref[0, rows, :] = (y * lax.rsqrt(ms + RMS_EPS) * nw_ref[0]).astype(BF16)
        return carry

    lax.fori_loop(0, nchunks, epilogue, 0)


def _ssd(proj3, dt3, cw_x, cw_bc, cb_x, cb_bc, dtb, alog, dskip, nw, tri, expand):
    b, s, _ = proj3.shape
    nchunks = s // SSD_CHUNK
    N = SSD_STATE
    seq = lambda width, col: pl.BlockSpec((1, s, width), lambda i, g, col=col, width=width: (i, 0, col // width + g))
    per_group = lambda shape: pl.BlockSpec((1,) + shape, lambda i, g: (g,) + (0,) * len(shape))
    return pl.pallas_call(
        functools.partial(_ssd_body, nchunks=nchunks),
        grid=(b, SSD_GROUPS),
        in_specs=[
            seq(SSD_GW, COL_X), seq(N, COL_B), seq(N, COL_C), seq(SSD_GW, COL_Z),
            pl.BlockSpec((1, s, LANES), lambda i, g: (i, 0, g)),
            per_group((SSD_CONV, SSD_GW)), per_group((SSD_CONV, 2 * N)),
            per_group((1, SSD_GW)), per_group((1, 2 * N)),
            per_group((1, LANES)), per_group((1, LANES)),
            per_group((1, SSD_GW)), per_group((1, SSD_GW)),
            pl.BlockSpec((SSD_CHUNK, SSD_CHUNK), lambda i, g: (0, 0)),
            pl.BlockSpec((2, LANES, SSD_GW), lambda i, g: (0, 0, 0)),
        ],
        out_specs=pl.BlockSpec((1, s, SSD_GW), lambda i, g: (i, 0, g)),
        out_shape=jax.ShapeDtypeStruct((b, s, SSD_WIDTH), BF16),
        scratch_shapes=[
            pltpu.VMEM((s, SSD_GW), BF16),
            pltpu.VMEM((s, N), BF16),
            pltpu.VMEM((s, N), BF16),
            pltpu.VMEM((s, SSD_GW), F32),
            pltpu.VMEM((2, N, SSD_GW), F32),
        ],
        compiler_params=pltpu.CompilerParams(
            dimension_semantics=("arbitrary", "arbitrary"),
            vmem_limit_bytes=VMEM_LIMIT_BYTES),
        name="ssd",
    )(proj3, proj3, proj3, proj3, dt3, cw_x, cw_bc, cb_x, cb_bc, dtb, alog, dskip, nw, tri, expand)


def _ssd_constants():
    tri = np.tril(np.ones((SSD_CHUNK, SSD_CHUNK), np.float32))
    expand = np.zeros((2, LANES, SSD_GW), np.float32)
    for d in range(2):
        for h in range(SSD_HPG):
            expand[d, d * SSD_HPG + h, h * SSD_HEAD_DIM:(h + 1) * SSD_HEAD_DIM] = 1.0
    return jnp.asarray(tri, BF16), jnp.asarray(expand, BF16)


FNET_CW = 256
FNET_N2 = GRID_W


def _fnet_body(u_ref, g_ref, wf_ref, cs_ref, r1_ref, e2_ref, o_ref, re_scr, im_scr, out_scr, *, n1):
    n2 = FNET_N2
    s = n1 * n2
    gd = FNET_GROUP_DIM
    ngrp = FNET_CW // gd
    tr = 512 if s % 512 == 0 else s

    def chan(t, carry):
        rows = pl.ds(pl.multiple_of(t * tr, tr), tr)
        for gi in range(ngrp):
            ri = _dot(u_ref[0, rows, gi * gd:(gi + 1) * gd], cs_ref[...])
            re_scr[gi, rows, :] = ri[:, :gd]
            im_scr[gi, rows, :] = ri[:, gd:]
        return carry

    lax.fori_loop(0, s // tr, chan, 0)

    def gather(scr, idx):
        return jnp.concatenate([scr[gi, idx, :] for gi in range(ngrp)], axis=1)

    def scatter(scr, idx, val):
        for gi in range(ngrp):
            scr[gi, idx, :] = val[:, gi * gd:(gi + 1) * gd]

    def stage1(c, carry):
        idx = pl.ds(c, n1, stride=n2)
        rhs = jnp.concatenate([gather(re_scr, idx), gather(im_scr, idx)], axis=0).astype(BF16)
        t = _dot(r1_ref[...], rhs)
        scatter(re_scr, idx, t[:n1])
        scatter(im_scr, idx, t[n1:])
        return carry

    lax.fori_loop(0, n2, stage1, 0)

    def stage2(k1, carry):
        blk = pl.ds(pl.multiple_of(k1 * n2, n2), n2)
        rhs = jnp.concatenate([gather(re_scr, blk), gather(im_scr, blk)], axis=0).astype(BF16)
        scatter(out_scr, pl.ds(k1, n2, stride=n1), _dot(e2_ref[blk, :], rhs))
        return carry

    lax.fori_loop(0, n1, stage2, 0)

    ortho = 1.0 / math.sqrt(s * gd)

    def mix(t, carry):
        rows = pl.ds(pl.multiple_of(t * tr, tr), tr)
        outs = [_dot(out_scr[gi, rows, :].astype(BF16), wf_ref[gi]) for gi in range(ngrp)]
        y = jnp.concatenate(outs, axis=1) * ortho
        o_ref[0, rows, :] = (y * _silu(g_ref[0, rows, :].astype(F32))).astype(BF16)
        return carry

    lax.fori_loop(0, s // tr, mix, 0)


def _fnet(proj3, w_f, cs_ch, r1, e2):
    b, s, _ = proj3.shape
    n1 = s // FNET_N2
    nblk = FNET_WIDTH // FNET_CW
    ngrp = FNET_CW // FNET_GROUP_DIM
    return pl.pallas_call(
        functools.partial(_fnet_body, n1=n1),
        grid=(b, nblk),
        in_specs=[
            pl.BlockSpec((1, s, FNET_CW), lambda i, j: (i, 0, COL_F // FNET_CW + j)),
            pl.BlockSpec((1, s, FNET_CW), lambda i, j: (i, 0, COL_GF // FNET_CW + j)),
            pl.BlockSpec((ngrp, FNET_GROUP_DIM, FNET_GROUP_DIM), lambda i, j: (j, 0, 0)),
            pl.BlockSpec((FNET_GROUP_DIM, 2 * FNET_GROUP_DIM), lambda i, j: (0, 0)),
            pl.BlockSpec((2 * n1, 2 * n1), lambda i, j: (0, 0)),
            pl.BlockSpec((s, 2 * FNET_N2), lambda i, j: (0, 0)),
        ],
        out_specs=pl.BlockSpec((1, s, FNET_CW), lambda i, j: (i, 0, j)),
        out_shape=jax.ShapeDtypeStruct((b, s, FNET_WIDTH), BF16),
        scratch_shapes=[pltpu.VMEM((ngrp, s, FNET_GROUP_DIM), F32)] * 3,
        compiler_params=pltpu.CompilerParams(
            dimension_semantics=("arbitrary", "arbitrary"),
            vmem_limit_bytes=VMEM_LIMIT_BYTES),
        name="fnet",
    )(proj3, proj3, w_f, cs_ch, r1, e2)


def _fnet_constants(s):
    n2 = FNET_N2
    n1 = s // n2
    gd = FNET_GROUP_DIM
    jk = np.outer(np.arange(gd), np.arange(gd)) * (2.0 * np.pi / gd)
    cs_ch = np.concatenate([np.cos(jk), -np.sin(jk)], axis=1)
    rk = np.outer(np.arange(n1), np.arange(n1)) * (2.0 * np.pi / n1)
    cr, sr = np.cos(rk), np.sin(rk)
    r1 = np.block([[cr, sr], [-sr, cr]])
    k1 = np.arange(n1)[:, None, None]
    k2 = np.arange(n2)[None, :, None]
    c = np.arange(n2)[None, None, :]
    theta = (2.0 * np.pi / s) * ((c * (k1 + n1 * k2)) % s)
    e2 = np.concatenate([np.cos(theta), np.sin(theta)], axis=2).reshape(n1 * n2, 2 * n2)
    return (jnp.asarray(cs_ch, BF16), jnp.asarray(r1, BF16), jnp.asarray(e2, BF16))


OUTPROJ_TM = 1024


def _outproj_body(x_ref, na_ref, ssd_ref, fn_ref, w_ref, fw_ref, o_ref, *, final):
    acc = _dot(na_ref[...], w_ref[0:NA_WIDTH, :])
    acc = acc + _dot(ssd_ref[...], w_ref[NA_WIDTH:NA_WIDTH + SSD_WIDTH, :])
    acc = acc + _dot(fn_ref[...], w_ref[NA_WIDTH + SSD_WIDTH:D_MIX, :])
    y = x_ref[...] + acc
    if final:
        ms = jnp.mean(y * y, axis=-1, keepdims=True)
        y = y * lax.rsqrt(ms + RMS_EPS) * fw_ref[...]
    o_ref[...] = y


def _outproj(x2, na2, ssd2, fn2, w_out, final_w, final):
    m = x2.shape[0]
    tm = min(OUTPROJ_TM, m)
    row = lambda width: pl.BlockSpec((tm, width), lambda i: (i, 0))
    return pl.pallas_call(
        functools.partial(_outproj_body, final=final),
        grid=(m // tm,),
        in_specs=[
            row(D_MODEL), row(NA_WIDTH), row(SSD_WIDTH), row(FNET_WIDTH),
            pl.BlockSpec((D_MIX, D_MODEL), lambda i: (0, 0)),
            pl.BlockSpec((1, D_MODEL), lambda i: (0, 0)),
        ],
        out_specs=row(D_MODEL),
        out_shape=jax.ShapeDtypeStruct((m, D_MODEL), F32),
        compiler_params=pltpu.CompilerParams(
            dimension_semantics=("arbitrary",),
            vmem_limit_bytes=VMEM_LIMIT_BYTES),
        name="outproj_final" if final else "outproj",
    )(x2, na2, ssd2, fn2, w_out, final_w.reshape(1, D_MODEL))


def _layer_params(w_in, conv_w, conv_b, dt_bias, a_log, d_skip, ssd_norm_w):
    dt0 = 4 * NA_WIDTH + SSD_WIDTH + (SSD_WIDTH + 2 * SSD_GROUPS * SSD_STATE)
    w_main = jnp.concatenate([w_in[:, :dt0], w_in[:, dt0 + 2 * SSD_HEADS:]], axis=1).astype(BF16)
    wdt = w_in[:, dt0:dt0 + 2 * SSD_HEADS].reshape(D_MODEL, 2, SSD_GROUPS, SSD_HPG)
    wdt = wdt.transpose(0, 2, 1, 3).reshape(D_MODEL, SSD_GROUPS, 2 * SSD_HPG)
    wdt = jnp.pad(wdt, ((0, 0), (0, 0), (0, LANES - 2 * SSD_HPG))).reshape(D_MODEL, DT_COLS).astype(BF16)

    def head_rows(v):
        v = v.reshape(2, SSD_GROUPS, SSD_HPG).transpose(1, 0, 2).reshape(SSD_GROUPS, 1, 2 * SSD_HPG)
        return jnp.pad(v, ((0, 0), (0, 0), (0, LANES - 2 * SSD_HPG)))

    N = SSD_STATE
    cw_x = conv_w[:, :SSD_WIDTH].reshape(SSD_CONV, SSD_GROUPS, SSD_GW).transpose(1, 0, 2)
    cb_x = conv_b[:SSD_WIDTH].reshape(SSD_GROUPS, 1, SSD_GW)
    cw_b = conv_w[:, SSD_WIDTH:SSD_WIDTH + SSD_GROUPS * N].reshape(SSD_CONV, SSD_GROUPS, N)
    cw_c = conv_w[:, SSD_WIDTH + SSD_GROUPS * N:].reshape(SSD_CONV, SSD_GROUPS, N)
    cw_bc = jnp.concatenate([cw_b, cw_c], axis=2).transpose(1, 0, 2)
    cb_b = conv_b[SSD_WIDTH:SSD_WIDTH + SSD_GROUPS * N].reshape(SSD_GROUPS, 1, N)
    cb_c = conv_b[SSD_WIDTH + SSD_GROUPS * N:].reshape(SSD_GROUPS, 1, N)
    cb_bc = jnp.concatenate([cb_b, cb_c], axis=2)
    dskip = jnp.repeat(d_skip, SSD_HEAD_DIM).reshape(SSD_GROUPS, 1, SSD_GW)
    nw = ssd_norm_w.reshape(SSD_GROUPS, 1, SSD_GW)
    return dict(w_main=w_main, w_dt=wdt, cw_x=cw_x, cw_bc=cw_bc, cb_x=cb_x, cb_bc=cb_bc,
                dtb=head_rows(dt_bias), alog=head_rows(a_log), dskip=dskip, nw=nw)


def kernel(x, norm_w, w_in, na_rpb, conv_w, conv_b, dt_bias, a_log, d_skip, ssd_norm_w,
           w_fourier, w_out, final_norm_w):
    b, s, d = x.shape
    depth = w_in.shape[0]
    tri, expand = _ssd_constants()
    cs_ch, r1, e2 = _fnet_constants(s)
    x2 = x.reshape(b * s, d)
    for i in range(depth):
        p = _layer_params(w_in[i], conv_w[i], conv_b[i], dt_bias[i], a_log[i], d_skip[i], ssd_norm_w[i])
        proj, dt = _inproj(x2, norm_w[i], p["w_main"], p["w_dt"])
        proj3 = proj.reshape(b, s, PROJ_COLS)
        dt3 = dt.reshape(b, s, DT_COLS)
        na = _natten(proj3, _natten_bias_table(na_rpb[i]))
        ssd = _ssd(proj3, dt3, p["cw_x"], p["cw_bc"], p["cb_x"], p["cb_bc"], p["dtb"], p["alog"],
                   p["dskip"], p["nw"], tri, expand)
        fn = _fnet(proj3, w_fourier[i].astype(BF16), cs_ch, r1, e2)
        x2 = _outproj(x2, na.reshape(b * s, NA_WIDTH), ssd.reshape(b * s, SSD_WIDTH),
                      fn.reshape(b * s, FNET_WIDTH), w_out[i].astype(BF16), final_norm_w,
                      final=(i == depth - 1))
    return x2.reshape(b, s, d)
```

```python
import functools
import math

import numpy as np
import jax
import jax.numpy as jnp
from jax import lax
from jax.experimental import pallas as pl
from jax.experimental.pallas import tpu as pltpu

F32 = jnp.float32
BF16 = jnp.bfloat16

D_MODEL = 1024
GRID_W = 64
NA_HEADS = 8
NA_HEAD_DIM = 64
NA_WIDTH = NA_HEADS * NA_HEAD_DIM
NA_WIN_H = 8
NA_WIN_W = 16
SSD_HEADS = 16
SSD_HEAD_DIM = 64
SSD_WIDTH = SSD_HEADS * SSD_HEAD_DIM
SSD_GROUPS = 2
SSD_HPG = SSD_HEADS // SSD_GROUPS
SSD_GW = SSD_HPG * SSD_HEAD_DIM
SSD_STATE = 128
SSD_CONV = 5
SSD_CHUNK = 128
FNET_GROUPS = 4
FNET_GROUP_DIM = 128
FNET_WIDTH = FNET_GROUPS * FNET_GROUP_DIM
D_MIX = NA_WIDTH + SSD_WIDTH + FNET_WIDTH
RMS_EPS = 1e-6

LANES = 128
VMEM_LIMIT_BYTES = 56 * 1024 * 1024

PROJ_COLS = 4 * NA_WIDTH + SSD_WIDTH + (SSD_WIDTH + 2 * SSD_GROUPS * SSD_STATE) + 2 * FNET_WIDTH
COL_Q, COL_K, COL_V, COL_GNA = 0, 512, 1024, 1536
COL_Z = 2048
COL_X = 3072
COL_B = 4096
COL_C = 4352
COL_F = 4608
COL_GF = 5120
DT_COLS = SSD_GROUPS * LANES

NEG_BIG = -1e30


def _silu(v):
    return v * (1.0 / (1.0 + jnp.exp(-v)))


def _softplus(v):
    return jnp.maximum(v, 0.0) + jnp.log(1.0 + jnp.exp(-jnp.abs(v)))


def _split3(v):
    hi = v.astype(BF16)
    r1 = v - hi.astype(F32)
    mid = r1.astype(BF16)
    lo = (r1 - mid.astype(F32)).astype(BF16)
    return hi, mid, lo


def _dot(a, b):
    return jnp.dot(a, b, preferred_element_type=F32)


def _dot_exact_rhs(a_f32, b_bf16):
    hi, mid, lo = _split3(a_f32)
    return _dot(hi, b_bf16) + _dot(mid, b_bf16) + _dot(lo, b_bf16)


def _dot_exact_lhs(a_bf16, b_f32):
    hi, mid, lo = _split3(b_f32)
    return _dot(a_bf16, hi) + _dot(a_bf16, mid) + _dot(a_bf16, lo)


INPROJ_TM = 1024
INPROJ_TN = PROJ_COLS // 4


def _inproj_body(x_ref, nw_ref, w_ref, wdt_ref, proj_ref, dt_ref, h_scr):
    @pl.when(pl.program_id(1) == 0)
    def _():
        x = x_ref[...]
        ms = jnp.mean(x * x, axis=-1, keepdims=True)
        h = (x * lax.rsqrt(ms + RMS_EPS) * nw_ref[...]).astype(BF16)
        h_scr[...] = h
        dt_ref[...] = _dot(h, wdt_ref[...])

    proj_ref[...] = _dot(h_scr[...], w_ref[...]).astype(BF16)


def _inproj(x2, norm_w, w_main, w_dt):
    m = x2.shape[0]
    tm = min(INPROJ_TM, m)
    return pl.pallas_call(
        _inproj_body,
        grid=(m // tm, PROJ_COLS // INPROJ_TN),
        in_specs=[
            pl.BlockSpec((tm, D_MODEL), lambda i, j: (i, 0)),
            pl.BlockSpec((1, D_MODEL), lambda i, j: (0, 0)),
            pl.BlockSpec((D_MODEL, INPROJ_TN), lambda i, j: (0, j)),
            pl.BlockSpec((D_MODEL, DT_COLS), lambda i, j: (0, 0)),
        ],
        out_specs=[
            pl.BlockSpec((tm, INPROJ_TN), lambda i, j: (i, j)),
            pl.BlockSpec((tm, DT_COLS), lambda i, j: (i, 0)),
        ],
        out_shape=[
            jax.ShapeDtypeStruct((m, PROJ_COLS), BF16),
            jax.ShapeDtypeStruct((m, DT_COLS), F32),
        ],
        scratch_shapes=[pltpu.VMEM((tm, D_MODEL), BF16)],
        compiler_params=pltpu.CompilerParams(
            dimension_semantics=("arbitrary", "arbitrary"),
            vmem_limit_bytes=VMEM_LIMIT_BYTES),
        name="inproj",
    )(x2, norm_w.reshape(1, D_MODEL), w_main, w_dt)


NA_ROWS_PER_STEP = 8
NA_ROWS_PER_ITER = 2
NA_PAIRS = NA_HEADS // 2
NA_BAND = NA_WIN_H * GRID_W


def _natten_body(q_ref, k_ref, v_ref, g_ref, bias_ref, o_ref, *, rows):
    rb = pl.program_id(1)
    lane = lax.broadcasted_iota(jnp.int32, (GRID_W, LANES), 1)
    first_head = lane < NA_HEAD_DIM
    scale = NA_HEAD_DIM ** -0.5

    def row_geometry(rr):
        r = rb * NA_ROWS_PER_STEP + rr
        r0 = jnp.clip(r - NA_WIN_H // 2, 0, rows - NA_WIN_H)
        return (r - r0, pl.multiple_of(r0 * GRID_W, GRID_W), pl.multiple_of(rr * GRID_W, GRID_W))

    def scores(geom):
        delta, kstart, qstart = geom
        logits = []
        for p in range(NA_PAIRS):
            cols = slice(p * LANES, (p + 1) * LANES)
            q2 = q_ref[0, pl.ds(qstart, GRID_W), cols] * scale
            zero = jnp.zeros_like(q2)
            qs = jnp.concatenate([jnp.where(first_head, q2, zero),
                                  jnp.where(first_head, zero, q2)], axis=0)
            kp = k_ref[0, pl.ds(kstart, NA_BAND), cols]
            lg = lax.dot_general(qs, kp, (((1,), (1,)), ((), ())),
                                 preferred_element_type=F32)
            logits.append(lg + bias_ref[delta, p])
        return logits

    def softmax_parts(logits):
        probs, dens = [], []
        for lg in logits:
            mx = jnp.max(lg, axis=-1, keepdims=True)
            e = jnp.exp(lg - mx)
            dens.append(jnp.sum(e, axis=-1, keepdims=True))
            probs.append(e.astype(BF16))
        return probs, dens

    def values(geom, probs, dens):
        _, kstart, qstart = geom
        outs = []
        for p in range(NA_PAIRS):
            vp = v_ref[0, pl.ds(kstart, NA_BAND), p * LANES:(p + 1) * LANES]
            o = _dot(probs[p], vp) / dens[p]
            outs.append(jnp.where(first_head, o[:GRID_W], o[GRID_W:]))
        att = jnp.concatenate(outs, axis=1)
        gate = g_ref[0, pl.ds(qstart, GRID_W), :].astype(F32)
        o_ref[0, pl.ds(qstart, GRID_W), :] = (att * _silu(gate)).astype(BF16)

    def row_group(i, carry):
        geoms = [row_geometry(i * NA_ROWS_PER_ITER + j) for j in range(NA_ROWS_PER_ITER)]
        logits = [scores(g) for g in geoms]
        for g, lg in zip(geoms, logits):
            values(g, *softmax_parts(lg))
        return carry

    lax.fori_loop(0, NA_ROWS_PER_STEP // NA_ROWS_PER_ITER, row_group, 0)


def _natten(proj3, bias_tab):
    b, s, _ = proj3.shape
    rows = s // GRID_W
    tq = NA_ROWS_PER_STEP * GRID_W
    blk = NA_WIDTH // 512
    return pl.pallas_call(
        functools.partial(_natten_body, rows=rows),
        grid=(b, rows // NA_ROWS_PER_STEP),
        in_specs=[
            pl.BlockSpec((1, tq, NA_WIDTH), lambda i, j: (i, j, COL_Q // NA_WIDTH)),
            pl.BlockSpec((1, s, NA_WIDTH), lambda i, j: (i, 0, COL_K // NA_WIDTH)),
            pl.BlockSpec((1, s, NA_WIDTH), lambda i, j: (i, 0, COL_V // NA_WIDTH)),
            pl.BlockSpec((1, tq, NA_WIDTH), lambda i, j: (i, j, COL_GNA // NA_WIDTH)),
            pl.BlockSpec((NA_WIN_H, NA_PAIRS, 2 * GRID_W, NA_BAND), lambda i, j: (0, 0, 0, 0)),
        ],
        out_specs=pl.BlockSpec((1, tq, NA_WIDTH), lambda i, j: (i, j, 0)),
        out_shape=jax.ShapeDtypeStruct((b, s, NA_WIDTH), BF16),
        compiler_params=pltpu.CompilerParams(
            dimension_semantics=("arbitrary", "arbitrary"),
            vmem_limit_bytes=VMEM_LIMIT_BYTES),
        name="natten",
    )(proj3, proj3, proj3, proj3, bias_tab)


def _natten_bias_table(rpb):
    qc = np.arange(GRID_W)[:, None]
    kc = np.arange(GRID_W)[None, :]
    col_start = np.clip(qc - NA_WIN_W // 2, 0, GRID_W - NA_WIN_W)
    inside = (kc >= col_start) & (kc < col_start + NA_WIN_W)
    col_rel = np.clip(kc - qc + (NA_WIN_W - 1), 0, 2 * NA_WIN_W - 2)
    onehot = (np.arange(2 * NA_WIN_W - 1)[None, None, :] == col_rel[:, :, None]) & inside[:, :, None]
    toe = jnp.einsum("hrj,qkj->hrqk", rpb.astype(F32), jnp.asarray(onehot, F32),
                     precision=lax.Precision.HIGHEST)
    toe = toe + jnp.asarray(np.where(inside, 0.0, NEG_BIG), F32)
    tab = jnp.stack([toe[:, NA_WIN_H - 1 - dl:2 * NA_WIN_H - 1 - dl] for dl in range(NA_WIN_H)])
    tab = tab.transpose(0, 1, 3, 2, 4)
    return tab.reshape(NA_WIN_H, NA_PAIRS, 2 * GRID_W, NA_BAND)


CONV_HALO = 16


def _conv_silu(u_ref, col0, width, w, bias, c, nchunks):
    L = SSD_CHUNK
    start = pl.multiple_of(c * L, L)
    prev_start = pl.multiple_of(jnp.maximum(c * L - CONV_HALO, 0), CONV_HALO)
    next_start = pl.multiple_of(jnp.minimum(c * L + L, (nchunks - 1) * L + L - CONV_HALO), CONV_HALO)
    cols = slice(col0, col0 + width)
    cur = u_ref[0, pl.ds(start, L), cols].astype(F32)
    prev = u_ref[0, pl.ds(prev_start, CONV_HALO), cols].astype(F32)
    nxt = u_ref[0, pl.ds(next_start, CONV_HALO), cols].astype(F32)
    prev = jnp.where(c > 0, prev, 0.0)
    nxt = jnp.where(c < nchunks - 1, nxt, 0.0)
    u = jnp.concatenate([prev, cur, nxt], axis=0)
    n = L + 2 * CONV_HALO
    pad = SSD_CONV // 2
    acc = bias
    for j in range(SSD_CONV):
        shift = (pad - j) % n
        shifted = u if shift == 0 else pltpu.roll(u, shift, 0)
        acc = acc + shifted[CONV_HALO:CONV_HALO + L] * w[j:j + 1, :]
    return _silu(acc)


def _ssd_body(x_ref, b_ref, c_ref, z_ref, dt_ref, cwx_ref, cwbc_ref, cbx_ref, cbbc_ref,
              dtb_ref, alog_ref, dskip_ref, nw_ref, tri_ref, exp_ref,
              o_ref, xc_scr, bc_scr, cc_scr, y_scr, st_scr, *, nchunks):
    L = SSD_CHUNK
    N = SSD_STATE

    def prologue(c, carry):
        rows = pl.ds(pl.multiple_of(c * L, L), L)
        xv = _conv_silu(x_ref, 0, SSD_GW, cwx_ref[0], cbx_ref[0], c, nchunks)
        xc_scr[rows, :] = xv.astype(BF16)
        y_scr[rows, :] = xv * dskip_ref[0]
        bv = _conv_silu(b_ref, 0, N, cwbc_ref[0, :, 0:N], cbbc_ref[0, :, 0:N], c, nchunks)
        bc_scr[rows, :] = bv.astype(BF16)
        cv = _conv_silu(c_ref, 0, N, cwbc_ref[0, :, N:2 * N], cbbc_ref[0, :, N:2 * N], c, nchunks)
        cc_scr[rows, :] = cv.astype(BF16)
        return carry

    lax.fori_loop(0, nchunks, prologue, 0)
    st_scr[...] = jnp.zeros_like(st_scr)

    a_row = -jnp.exp(alog_ref[0])
    dtb_row = dtb_ref[0]
    tri = tri_ref[...]
    li = lax.broadcasted_iota(jnp.int32, (L, L), 0)
    si = lax.broadcasted_iota(jnp.int32, (L, L), 1)
    lane = lax.broadcasted_iota(jnp.int32, (L, LANES), 1)
    first_head = lane < SSD_HEAD_DIM

    def cumsums(c):
        rows = pl.ds(pl.multiple_of(c * L, L), L)
        dt = _softplus(dt_ref[0, rows, :] + dtb_row)
        adt = dt * a_row
        cs = _dot_exact_lhs(tri, adt)
        return rows, dt, adt, cs

    def weights(d, rows, dt, adt, cs):
        tot = cs[L - 1:L, :]
        if d == 0:
            u = cs
            st_w = jnp.exp(tot - cs)
        else:
            u = adt - cs
            st_w = jnp.exp(-u)
        expand = exp_ref[d]
        dtw_e = _dot((dt * st_w).astype(BF16), expand).astype(BF16)
        wts = xc_scr[rows, :] * dtw_e
        decay = jnp.exp(jnp.broadcast_to(tot, (8, LANES)))
        decay_e = _dot_exact_rhs(decay, expand)[0:1, :]
        return u, u.T, dt.T, tot, wts, decay_e

    def outputs(d, rows, u, u_t, dt_t, tot):
        bcv = bc_scr[rows, :]
        ccv = cc_scr[rows, :]
        g = lax.dot_general(ccv, bcv, (((1,), (1,)), ((), ())), preferred_element_type=F32)
        mask = (li >= si) if d == 0 else (si >= li)
        parts = []
        for p in range(SSD_HPG // 2):
            cols = slice(p * LANES, (p + 1) * LANES)
            rhs = jnp.concatenate([xc_scr[rows, cols], st_scr[d, :, cols].astype(BF16)], axis=0)
            lhs = []
            for hh in range(2):
                col = d * SSD_HPG + 2 * p + hh
                ucol = jnp.broadcast_to(u[:, col:col + 1], (L, L))
                dec = jnp.exp(jnp.where(mask, ucol - u_t[col:col + 1, :], NEG_BIG))
                m = (g * dec * dt_t[col:col + 1, :]).astype(BF16)
                if d == 0:
                    into = jnp.exp(ucol)
                else:
                    into = jnp.exp(ucol + tot[:, col:col + 1])
                lhs.append(jnp.concatenate([m, ccv * into.astype(BF16)], axis=1))
            y2 = _dot(jnp.concatenate(lhs, axis=0), rhs)
            parts.append(jnp.where(first_head, y2[:L], y2[L:]))
        y_scr[rows, :] += jnp.concatenate(parts, axis=1)

    def advance_state(d, rows, wts, decay_e):
        st_scr[d] = st_scr[d] * decay_e + lax.dot_general(
            bc_scr[rows, :], wts, (((0,), (0,)), ((), ())), preferred_element_type=F32)

    def main(i, carry):
        sums = [cumsums(i), cumsums(nchunks - 1 - i)]
        wgt = [weights(d, *sums[d]) for d in range(2)]
        for d in range(2):
            u, u_t, dt_t, tot, _, _ = wgt[d]
            outputs(d, sums[d][0], u, u_t, dt_t, tot)
        for d in range(2):
            advance_state(d, sums[d][0], wgt[d][4], wgt[d][5])
        return carry

    lax.fori_loop(0, nchunks, main, 0)

    def epilogue(c, carry):
        rows = pl.ds(pl.multiple_of(c * L, L), L)
        y = y_scr[rows, :] * _silu(z_ref[0, rows, :].astype(F32))
        ms = jnp.mean(y * y, axis=-1, keepdims=True)
        o_ref[0, rows, :] = (y * lax.rsqrt(ms + RMS_EPS) * nw_ref[0]).astype(BF16)
        return carry

    lax.fori_loop(0, nchunks, epilogue, 0)


def _ssd(proj3, dt3, cw_x, cw_bc, cb_x, cb_bc, dtb, alog, dskip, nw, tri, expand):
    b, s, _ = proj3.shape
    nchunks = s // SSD_CHUNK
    N = SSD_STATE
    seq = lambda width, col: pl.BlockSpec((1, s, width), lambda i, g, col=col, width=width: (i, 0, col // width + g))
    per_group = lambda shape: pl.BlockSpec((1,) + shape, lambda i, g: (g,) + (0,) * len(shape))
    return pl.pallas_call(
        functools.partial(_ssd_body, nchunks=nchunks),
        grid=(b, SSD_GROUPS),
        in_specs=[
            seq(SSD_GW, COL_X), seq(N, COL_B), seq(N, COL_C), seq(SSD_GW, COL_Z),
            pl.BlockSpec((1, s, LANES), lambda i, g: (i, 0, g)),
            per_group((SSD_CONV, SSD_GW)), per_group((SSD_CONV, 2 * N)),
            per_group((1, SSD_GW)), per_group((1, 2 * N)),
            per_group((1, LANES)), per_group((1, LANES)),
            per_group((1, SSD_GW)), per_group((1, SSD_GW)),
            pl.BlockSpec((SSD_CHUNK, SSD_CHUNK), lambda i, g: (0, 0)),
            pl.BlockSpec((2, LANES, SSD_GW), lambda i, g: (0, 0, 0)),
        ],
        out_specs=pl.BlockSpec((1, s, SSD_GW), lambda i, g: (i, 0, g)),
        out_shape=jax.ShapeDtypeStruct((b, s, SSD_WIDTH), BF16),
        scratch_shapes=[
            pltpu.VMEM((s, SSD_GW), BF16),
            pltpu.VMEM((s, N), BF16),
            pltpu.VMEM((s, N), BF16),
            pltpu.VMEM((s, SSD_GW), F32),
            pltpu.VMEM((2, N, SSD_GW), F32),
        ],
        compiler_params=pltpu.CompilerParams(
            dimension_semantics=("arbitrary", "arbitrary"),
            vmem_limit_bytes=VMEM_LIMIT_BYTES),
        name="ssd",
    )(proj3, proj3, proj3, proj3, dt3, cw_x, cw_bc, cb_x, cb_bc, dtb, alog, dskip, nw, tri, expand)


def _ssd_constants():
    tri = np.tril(np.ones((SSD_CHUNK, SSD_CHUNK), np.float32))
    expand = np.zeros((2, LANES, SSD_GW), np.float32)
    for d in range(2):
        for h in range(SSD_HPG):
            expand[d, d * SSD_HPG + h, h * SSD_HEAD_DIM:(h + 1) * SSD_HEAD_DIM] = 1.0
    return jnp.asarray(tri, BF16), jnp.asarray(expand, BF16)


FNET_CW = 256
FNET_N2 = GRID_W
FNET_SUB = 8


def _fnet_body(u_ref, g_ref, wf_ref, cs_ref, k1_ref, e2_ref, o_ref, re_scr, im_scr, out_scr, *, n1):
    n2 = FNET_N2
    s = n1 * n2
    gd = FNET_GROUP_DIM
    ngrp = FNET_CW // gd
    sub = FNET_SUB
    tr = sub * n2

    def load(scr, idx):
        parts = []
        for gi in range(ngrp):
            v = scr.at[gi][idx]
            parts.append(v.reshape(v.shape[0] * v.shape[1], gd))
        return jnp.concatenate(parts, axis=1)

    def store(scr, idx, val, shape):
        for gi in range(ngrp):
            scr.at[gi][idx] = val[:, gi * gd:(gi + 1) * gd].reshape(shape)

    def chan(t, carry):
        rows = pl.ds(pl.multiple_of(t * tr, tr), tr)
        blk = (pl.ds(pl.multiple_of(t * sub, sub), sub), slice(None), slice(None))
        for gi in range(ngrp):
            ri = _dot(u_ref[0, rows, gi * gd:(gi + 1) * gd], cs_ref[...])
            re_scr.at[gi][blk] = ri[:, :gd].reshape(sub, n2, gd)
            im_scr.at[gi][blk] = ri[:, gd:].reshape(sub, n2, gd)
        return carry

    lax.fori_loop(0, n1 // sub, chan, 0)

    def stage1(j, carry):
        idx = (slice(None), pl.ds(pl.multiple_of(j * sub, sub), sub), slice(None))
        rhs = jnp.concatenate([load(re_scr, idx), load(im_scr, idx)], axis=0).astype(BF16)
        t = _dot(k1_ref[...], rhs)
        store(re_scr, idx, t[:sub * n1], (n1, sub, gd))
        store(im_scr, idx, t[sub * n1:], (n1, sub, gd))
        return carry

    lax.fori_loop(0, n2 // sub, stage1, 0)

    def stage2(j, carry):
        src = (pl.ds(pl.multiple_of(j * sub, sub), sub), slice(None), slice(None))
        dst = (slice(None), pl.ds(pl.multiple_of(j * sub, sub), sub), slice(None))
        rhs = jnp.concatenate([load(re_scr, src), load(im_scr, src)], axis=0).astype(BF16)
        store(out_scr, dst, _dot(e2_ref[j], rhs), (n2, sub, gd))
        return carry

    lax.fori_loop(0, n1 // sub, stage2, 0)

    ortho = 1.0 / math.sqrt(s * gd)

    tm = sub * n1

    def mix(t, carry):
        rows = pl.ds(pl.multiple_of(t * tm, tm), tm)
        blk = (pl.ds(pl.multiple_of(t * sub, sub), sub), slice(None), slice(None))
        outs = [_dot(out_scr.at[gi][blk].reshape(tm, gd).astype(BF16), wf_ref[gi]) for gi in range(ngrp)]
        y = jnp.concatenate(outs, axis=1) * ortho
        o_ref[0, rows, :] = (y * _silu(g_ref[0, rows, :].astype(F32))).astype(BF16)
        return carry

    lax.fori_loop(0, n2 // sub, mix, 0)


def _fnet(proj3, w_f, cs_ch, k1m, e2):
    b, s, _ = proj3.shape
    n2 = FNET_N2
    n1 = s // n2
    nblk = FNET_WIDTH // FNET_CW
    ngrp = FNET_CW // FNET_GROUP_DIM
    const = lambda shape: pl.BlockSpec(shape, lambda i, j: (0,) * len(shape))
    return pl.pallas_call(
        functools.partial(_fnet_body, n1=n1),
        grid=(b, nblk),
        in_specs=[
            pl.BlockSpec((1, s, FNET_CW), lambda i, j: (i, 0, COL_F // FNET_CW + j)),
            pl.BlockSpec((1, s, FNET_CW), lambda i, j: (i, 0, COL_GF // FNET_CW + j)),
            pl.BlockSpec((ngrp, FNET_GROUP_DIM, FNET_GROUP_DIM), lambda i, j: (j, 0, 0)),
            const(cs_ch.shape), const(k1m.shape), const(e2.shape),
        ],
        out_specs=pl.BlockSpec((1, s, FNET_CW), lambda i, j: (i, 0, j)),
        out_shape=jax.ShapeDtypeStruct((b, s, FNET_WIDTH), BF16),
        scratch_shapes=[pltpu.VMEM((ngrp, n1, n2, FNET_GROUP_DIM), F32),
                        pltpu.VMEM((ngrp, n1, n2, FNET_GROUP_DIM), F32),
                        pltpu.VMEM((ngrp, n2, n1, FNET_GROUP_DIM), F32)],
        compiler_params=pltpu.CompilerParams(
            dimension_semantics=("arbitrary", "arbitrary"),
            vmem_limit_bytes=VMEM_LIMIT_BYTES),
        name="fnet",
    )(proj3, proj3, w_f, cs_ch, k1m, e2)


def _fnet_constants(s):
    n2 = FNET_N2
    n1 = s // n2
    gd = FNET_GROUP_DIM
    sub = FNET_SUB
    jk = np.outer(np.arange(gd), np.arange(gd)) * (2.0 * np.pi / gd)
    cs_ch = np.concatenate([np.cos(jk), -np.sin(jk)], axis=1)
    rk = np.outer(np.arange(n1), np.arange(n1)) * (2.0 * np.pi / n1)
    eye = np.eye(sub)
    cr, sr = np.kron(np.cos(rk), eye), np.kron(np.sin(rk), eye)
    k1m = np.block([[cr, sr], [-sr, cr]])
    j = np.arange(n1 // sub)[:, None, None, None, None]
    k2 = np.arange(n2)[None, :, None, None, None]
    a = np.arange(sub)[None, None, :, None, None]
    a2 = np.arange(sub)[None, None, None, :, None]
    c = np.arange(n2)[None, None, None, None, :]
    theta = (2.0 * np.pi / s) * ((c * (sub * j + a + n1 * k2)) % s)
    same = (a == a2)
    e2 = np.concatenate([(np.cos(theta) * same).reshape(n1 // sub, n2 * sub, sub * n2),
                         (np.sin(theta) * same).reshape(n1 // sub, n2 * sub, sub * n2)], axis=2)
    return (jnp.asarray(cs_ch, BF16), jnp.asarray(k1m, BF16), jnp.asarray(e2, BF16))


OUTPROJ_TM = 1024


def _outproj_body(x_ref, na_ref, ssd_ref, fn_ref, w_ref, fw_ref, o_ref, *, final):
    acc = _dot(na_ref[...], w_ref[0:NA_WIDTH, :])
    acc = acc + _dot(ssd_ref[...], w_ref[NA_WIDTH:NA_WIDTH + SSD_WIDTH, :])
    acc = acc + _dot(fn_ref[...], w_ref[NA_WIDTH + SSD_WIDTH:D_MIX, :])
    y = x_ref[...] + acc
    if final:
        ms = jnp.mean(y * y, axis=-1, keepdims=True)
        y = y * lax.rsqrt(ms + RMS_EPS) * fw_ref[...]
    o_ref[...] = y


def _outproj(x2, na2, ssd2, fn2, w_out, final_w, final):
    m = x2.shape[0]
    tm = min(OUTPROJ_TM, m)
    row = lambda width: pl.BlockSpec((tm, width), lambda i: (i, 0))
    return pl.pallas_call(
        functools.partial(_outproj_body, final=final),
        grid=(m // tm,),
        in_specs=[
            row(D_MODEL), row(NA_WIDTH), row(SSD_WIDTH), row(FNET_WIDTH),
            pl.BlockSpec((D_MIX, D_MODEL), lambda i: (0, 0)),
            pl.BlockSpec((1, D_MODEL), lambda i: (0, 0)),
        ],
        out_specs=row(D_MODEL),
        out_shape=jax.ShapeDtypeStruct((m, D_MODEL), F32),
        compiler_params=pltpu.CompilerParams(
            dimension_semantics=("arbitrary",),
            vmem_limit_bytes=VMEM_LIMIT_BYTES),
        name="outproj_final" if final else "outproj",
    )(x2, na2, ssd2, fn2, w_out, final_w.reshape(1, D_MODEL))


def _layer_params(w_in, conv_w, conv_b, dt_bias, a_log, d_skip, ssd_norm_w):
    dt0 = 4 * NA_WIDTH + SSD_WIDTH + (SSD_WIDTH + 2 * SSD_GROUPS * SSD_STATE)
    w_main = jnp.concatenate([w_in[:, :dt0], w_in[:, dt0 + 2 * SSD_HEADS:]], axis=1).astype(BF16)
    wdt = w_in[:, dt0:dt0 + 2 * SSD_HEADS].reshape(D_MODEL, 2, SSD_GROUPS, SSD_HPG)
    wdt = wdt.transpose(0, 2, 1, 3).reshape(D_MODEL, SSD_GROUPS, 2 * SSD_HPG)
    wdt = jnp.pad(wdt, ((0, 0), (0, 0), (0, LANES - 2 * SSD_HPG))).reshape(D_MODEL, DT_COLS).astype(BF16)

    def head_rows(v):
        v = v.reshape(2, SSD_GROUPS, SSD_HPG).transpose(1, 0, 2).reshape(SSD_GROUPS, 1, 2 * SSD_HPG)
        return jnp.pad(v, ((0, 0), (0, 0), (0, LANES - 2 * SSD_HPG)))

    N = SSD_STATE
    cw_x = conv_w[:, :SSD_WIDTH].reshape(SSD_CONV, SSD_GROUPS, SSD_GW).transpose(1, 0, 2)
    cb_x = conv_b[:SSD_WIDTH].reshape(SSD_GROUPS, 1, SSD_GW)
    cw_b = conv_w[:, SSD_WIDTH:SSD_WIDTH + SSD_GROUPS * N].reshape(SSD_CONV, SSD_GROUPS, N)
    cw_c = conv_w[:, SSD_WIDTH + SSD_GROUPS * N:].reshape(SSD_CONV, SSD_GROUPS, N)
    cw_bc = jnp.concatenate([cw_b, cw_c], axis=2).transpose(1, 0, 2)
    cb_b = conv_b[SSD_WIDTH:SSD_WIDTH + SSD_GROUPS * N].reshape(SSD_GROUPS, 1, N)
    cb_c = conv_b[SSD_WIDTH + SSD_GROUPS * N:].reshape(SSD_GROUPS, 1, N)
    cb_bc = jnp.concatenate([cb_b, cb_c], axis=2)
    dskip = jnp.repeat(d_skip, SSD_HEAD_DIM).reshape(SSD_GROUPS, 1, SSD_GW)
    nw = ssd_norm_w.reshape(SSD_GROUPS, 1, SSD_GW)
    return dict(w_main=w_main, w_dt=wdt, cw_x=cw_x, cw_bc=cw_bc, cb_x=cb_x, cb_bc=cb_bc,
                dtb=head_rows(dt_bias), alog=head_rows(a_log), dskip=dskip, nw=nw)


def kernel(x, norm_w, w_in, na_rpb, conv_w, conv_b, dt_bias, a_log, d_skip, ssd_norm_w,
           w_fourier, w_out, final_norm_w):
    b, s, d = x.shape
    depth = w_in.shape[0]
    tri, expand = _ssd_constants()
    cs_ch, r1, e2 = _fnet_constants(s)
    x2 = x.reshape(b * s, d)
    for i in range(depth):
        p = _layer_params(w_in[i], conv_w[i], conv_b[i], dt_bias[i], a_log[i], d_skip[i], ssd_norm_w[i])
        proj, dt = _inproj(x2, norm_w[i], p["w_main"], p["w_dt"])
        proj3 = proj.reshape(b, s, PROJ_COLS)
        dt3 = dt.reshape(b, s, DT_COLS)
        na = _natten(proj3, _natten_bias_table(na_rpb[i]))
        ssd = _ssd(proj3, dt3, p["cw_x"], p["cw_bc"], p["cb_x"], p["cb_bc"], p["dtb"], p["alog"],
                   p["dskip"], p["nw"], tri, expand)
        fn = _fnet(proj3, w_fourier[i].astype(BF16), cs_ch, r1, e2)
        x2 = _outproj(x2, na.reshape(b * s, NA_WIDTH), ssd.reshape(b * s, SSD_WIDTH),
                      fn.reshape(b * s, FNET_WIDTH), w_out[i].astype(BF16), final_norm_w,
                      final=(i == depth - 1))
    return x2.reshape(b, s, d)
```

```python
import functools
import math

import numpy as np
import jax
import jax.numpy as jnp
from jax import lax
from jax.experimental import pallas as pl
from jax.experimental.pallas import tpu as pltpu

F32 = jnp.float32
BF16 = jnp.bfloat16

D_MODEL = 1024
GRID_W = 64
NA_HEADS = 8
NA_HEAD_DIM = 64
NA_WIDTH = NA_HEADS * NA_HEAD_DIM
NA_WIN_H = 8
NA_WIN_W = 16
SSD_HEADS = 16
SSD_HEAD_DIM = 64
SSD_WIDTH = SSD_HEADS * SSD_HEAD_DIM
SSD_GROUPS = 2
SSD_HPG = SSD_HEADS // SSD_GROUPS
SSD_GW = SSD_HPG * SSD_HEAD_DIM
SSD_STATE = 128
SSD_CONV = 5
SSD_CHUNK = 128
FNET_GROUPS = 4
FNET_GROUP_DIM = 128
FNET_WIDTH = FNET_GROUPS * FNET_GROUP_DIM
D_MIX = NA_WIDTH + SSD_WIDTH + FNET_WIDTH
RMS_EPS = 1e-6

LANES = 128
VMEM_LIMIT_BYTES = 56 * 1024 * 1024
SSD_VMEM_LIMIT_BYTES = 60 * 1024 * 1024

PROJ_COLS = 4 * NA_WIDTH + SSD_WIDTH + (SSD_WIDTH + 2 * SSD_GROUPS * SSD_STATE) + 2 * FNET_WIDTH
COL_Q, COL_K, COL_V, COL_GNA = 0, 512, 1024, 1536
COL_Z = 2048
COL_X = 3072
COL_B = 4096
COL_C = 4352
COL_F = 4608
COL_GF = 5120
DT_COLS = SSD_GROUPS * LANES

NEG_BIG = -1e30
LOG2E = math.log2(math.e)


def _silu(v):
    return v * (1.0 / (1.0 + jnp.exp(-v)))


def _softplus(v):
    return jnp.maximum(v, 0.0) + jnp.log(1.0 + jnp.exp(-jnp.abs(v)))


def _split3(v):
    hi = v.astype(BF16)
    r1 = v - hi.astype(F32)
    mid = r1.astype(BF16)
    lo = (r1 - mid.astype(F32)).astype(BF16)
    return hi, mid, lo


def _dot(a, b):
    return jnp.dot(a, b, preferred_element_type=F32)


def _dot_exact_rhs(a_f32, b_bf16):
    hi, mid, lo = _split3(a_f32)
    return _dot(hi, b_bf16) + _dot(mid, b_bf16) + _dot(lo, b_bf16)


def _dot_exact_lhs(a_bf16, b_f32):
    hi, mid, lo = _split3(b_f32)
    return _dot(a_bf16, hi) + _dot(a_bf16, mid) + _dot(a_bf16, lo)


INPROJ_TM = 512
INPROJ_TN = PROJ_COLS // 4


def _inproj_body(x_ref, nw_ref, w_ref, wdt_ref, proj_ref, dt_ref):
    x = x_ref[...]
    ms = jnp.mean(x * x, axis=-1, keepdims=True)
    h = (x * lax.rsqrt(ms + RMS_EPS) * nw_ref[...]).astype(BF16)
    dt_ref[...] = _dot(h, wdt_ref[...])
    for j in range(PROJ_COLS // INPROJ_TN):
        cols = slice(j * INPROJ_TN, (j + 1) * INPROJ_TN)
        proj_ref[:, cols] = _dot(h, w_ref[:, cols]).astype(BF16)


def _inproj(x2, norm_w, w_main, w_dt):
    m = x2.shape[0]
    tm = min(INPROJ_TM, m)
    return pl.pallas_call(
        _inproj_body,
        grid=(m // tm,),
        in_specs=[
            pl.BlockSpec((tm, D_MODEL), lambda i: (i, 0)),
            pl.BlockSpec((1, D_MODEL), lambda i: (0, 0)),
            pl.BlockSpec((D_MODEL, PROJ_COLS), lambda i: (0, 0)),
            pl.BlockSpec((D_MODEL, DT_COLS), lambda i: (0, 0)),
        ],
        out_specs=[
            pl.BlockSpec((tm, PROJ_COLS), lambda i: (i, 0)),
            pl.BlockSpec((tm, DT_COLS), lambda i: (i, 0)),
        ],
        out_shape=[
            jax.ShapeDtypeStruct((m, PROJ_COLS), BF16),
            jax.ShapeDtypeStruct((m, DT_COLS), F32),
        ],
        compiler_params=pltpu.CompilerParams(
            dimension_semantics=("arbitrary",),
            vmem_limit_bytes=VMEM_LIMIT_BYTES),
        name="inproj",
    )(x2, norm_w.reshape(1, D_MODEL), w_main, w_dt)


NA_ROWS_PER_STEP = 8
NA_ROWS_PER_ITER = 2
NA_PAIRS = NA_HEADS // 2
NA_BAND = NA_WIN_H * GRID_W


def _natten_body(q_ref, k_ref, v_ref, g_ref, bias_ref, o_ref, *, rows):
    rb = pl.program_id(1)
    lane = lax.broadcasted_iota(jnp.int32, (GRID_W, LANES), 1)
    first_head = lane < NA_HEAD_DIM
    scale = NA_HEAD_DIM ** -0.5

    def row_geometry(rr):
        r = rb * NA_ROWS_PER_STEP + rr
        r0 = jnp.clip(r - NA_WIN_H // 2, 0, rows - NA_WIN_H)
        return (r - r0, pl.multiple_of(r0 * GRID_W, GRID_W), pl.multiple_of(rr * GRID_W, GRID_W))

    def scores(geom):
        delta, kstart, qstart = geom
        logits = []
        for p in range(NA_PAIRS):
            cols = slice(p * LANES, (p + 1) * LANES)
            q2 = q_ref[0, pl.ds(qstart, GRID_W), cols] * scale
            zero = jnp.zeros_like(q2)
            qs = jnp.concatenate([jnp.where(first_head, q2, zero),
                                  jnp.where(first_head, zero, q2)], axis=0)
            kp = k_ref[0, pl.ds(kstart, NA_BAND), cols]
            lg = lax.dot_general(qs, kp, (((1,), (1,)), ((), ())),
                                 preferred_element_type=F32)
            logits.append(lg + bias_ref[delta, p])
        return logits

    def softmax_parts(logits):
        probs, dens = [], []
        for lg in logits:
            mx = jnp.max(lg, axis=-1, keepdims=True)
            e = jnp.exp(lg - mx)
            dens.append(jnp.sum(e, axis=-1, keepdims=True))
            probs.append(e.astype(BF16))
        return probs, dens

    def values(geom, probs, dens):
        _, kstart, qstart = geom
        outs = []
        for p in range(NA_PAIRS):
            vp = v_ref[0, pl.ds(kstart, NA_BAND), p * LANES:(p + 1) * LANES]
            o = _dot(probs[p], vp) / dens[p]
            outs.append(jnp.where(first_head, o[:GRID_W], o[GRID_W:]))
        att = jnp.concatenate(outs, axis=1)
        gate = g_ref[0, pl.ds(qstart, GRID_W), :].astype(F32)
        o_ref[0, pl.ds(qstart, GRID_W), :] = (att * _silu(gate)).astype(BF16)

    def row_group(i, carry):
        geoms = [row_geometry(i * NA_ROWS_PER_ITER + j) for j in range(NA_ROWS_PER_ITER)]
        logits = [scores(g) for g in geoms]
        for g, lg in zip(geoms, logits):
            values(g, *softmax_parts(lg))
        return carry

    lax.fori_loop(0, NA_ROWS_PER_STEP // NA_ROWS_PER_ITER, row_group, 0)


def _natten(proj3, bias_tab):
    b, s, _ = proj3.shape
    rows = s // GRID_W
    tq = NA_ROWS_PER_STEP * GRID_W
    blk = NA_WIDTH // 512
    return pl.pallas_call(
        functools.partial(_natten_body, rows=rows),
        grid=(b, rows // NA_ROWS_PER_STEP),
        in_specs=[
            pl.BlockSpec((1, tq, NA_WIDTH), lambda i, j: (i, j, COL_Q // NA_WIDTH)),
            pl.BlockSpec((1, s, NA_WIDTH), lambda i, j: (i, 0, COL_K // NA_WIDTH)),
            pl.BlockSpec((1, s, NA_WIDTH), lambda i, j: (i, 0, COL_V // NA_WIDTH)),
            pl.BlockSpec((1, tq, NA_WIDTH), lambda i, j: (i, j, COL_GNA // NA_WIDTH)),
            pl.BlockSpec((NA_WIN_H, NA_PAIRS, 2 * GRID_W, NA_BAND), lambda i, j: (0, 0, 0, 0)),
        ],
        out_specs=pl.BlockSpec((1, tq, NA_WIDTH), lambda i, j: (i, j, 0)),
        out_shape=jax.ShapeDtypeStruct((b, s, NA_WIDTH), BF16),
        compiler_params=pltpu.CompilerParams(
            dimension_semantics=("arbitrary", "arbitrary"),
            vmem_limit_bytes=VMEM_LIMIT_BYTES),
        name="natten",
    )(proj3, proj3, proj3, proj3, bias_tab)


def _natten_bias_table(rpb):
    qc = np.arange(GRID_W)[:, None]
    kc = np.arange(GRID_W)[None, :]
    col_start = np.clip(qc - NA_WIN_W // 2, 0, GRID_W - NA_WIN_W)
    inside = (kc >= col_start) & (kc < col_start + NA_WIN_W)
    col_rel = np.clip(kc - qc + (NA_WIN_W - 1), 0, 2 * NA_WIN_W - 2)
    onehot = (np.arange(2 * NA_WIN_W - 1)[None, None, :] == col_rel[:, :, None]) & inside[:, :, None]
    toe = jnp.einsum("hrj,qkj->hrqk", rpb.astype(F32), jnp.asarray(onehot, F32),
                     precision=lax.Precision.HIGHEST)
    toe = toe + jnp.asarray(np.where(inside, 0.0, NEG_BIG), F32)
    tab = jnp.stack([toe[:, NA_WIN_H - 1 - dl:2 * NA_WIN_H - 1 - dl] for dl in range(NA_WIN_H)])
    tab = tab.transpose(0, 1, 3, 2, 4)
    return tab.reshape(NA_WIN_H, NA_PAIRS, 2 * GRID_W, NA_BAND)


CONV_HALO = 16
SSD_EXP_UNROLL = 4


def _conv_silu(u_ref, col0, width, w, bias, c, nchunks):
    L = SSD_CHUNK
    start = pl.multiple_of(c * L, L)
    prev_start = pl.multiple_of(jnp.maximum(c * L - CONV_HALO, 0), CONV_HALO)
    next_start = pl.multiple_of(jnp.minimum(c * L + L, (nchunks - 1) * L + L - CONV_HALO), CONV_HALO)
    cols = slice(col0, col0 + width)
    cur = u_ref[0, pl.ds(start, L), cols].astype(F32)
    prev = u_ref[0, pl.ds(prev_start, CONV_HALO), cols].astype(F32)
    nxt = u_ref[0, pl.ds(next_start, CONV_HALO), cols].astype(F32)
    prev = jnp.where(c > 0, prev, 0.0)
    nxt = jnp.where(c < nchunks - 1, nxt, 0.0)
    u = jnp.concatenate([prev, cur, nxt], axis=0)
    n = L + 2 * CONV_HALO
    pad = SSD_CONV // 2
    acc = bias
    for j in range(SSD_CONV):
        shift = (pad - j) % n
        shifted = u if shift == 0 else pltpu.roll(u, shift, 0)
        acc = acc + shifted[CONV_HALO:CONV_HALO + L] * w[j:j + 1, :]
    return _silu(acc)


def _ssd_body(x_ref, b_ref, c_ref, z_ref, dt_ref, cwx_ref, cwbc_ref, cbx_ref, cbbc_ref,
              dtb_ref, alog_ref, dskip_ref, nw_ref, tri_ref, exp_ref,
              o_ref, xc_scr, bc_scr, cc_scr, y_scr, st_scr, u_scr, vt_scr, dtw_scr, lhs_scr, rhs_scr,
              *, nchunks):
    L = SSD_CHUNK
    N = SSD_STATE

    def prologue(c, carry):
        rows = pl.ds(pl.multiple_of(c * L, L), L)
        xv = _conv_silu(x_ref, 0, SSD_GW, cwx_ref[0], cbx_ref[0], c, nchunks)
        xc_scr[rows, :] = xv.astype(BF16)
        y_scr[rows, :] = xv * dskip_ref[0]
        bv = _conv_silu(b_ref, 0, N, cwbc_ref[0, :, 0:N], cbbc_ref[0, :, 0:N], c, nchunks)
        bc_scr[rows, :] = bv.astype(BF16)
        cv = _conv_silu(c_ref, 0, N, cwbc_ref[0, :, N:2 * N], cbbc_ref[0, :, N:2 * N], c, nchunks)
        cc_scr[rows, :] = cv.astype(BF16)
        return carry

    lax.fori_loop(0, nchunks, prologue, 0)

    a_row = -jnp.exp(alog_ref[0])
    dtb_row = dtb_ref[0]
    li = lax.broadcasted_iota(jnp.int32, (L, L), 0)
    si = lax.broadcasted_iota(jnp.int32, (L, L), 1)
    lane = lax.broadcasted_iota(jnp.int32, (L, LANES), 1)
    first_head = lane < SSD_HEAD_DIM
    fwd_lane = lane < SSD_HPG

    def exponents(i, carry):
        group = [i * SSD_EXP_UNROLL + j for j in range(SSD_EXP_UNROLL)]
        rows = [pl.ds(pl.multiple_of(c * L, L), L) for c in group]
        dts = [_softplus(dt_ref[0, r, :] + dtb_row) for r in rows]
        adts = [dt * a_row for dt in dts]
        prefix = [_dot_exact_lhs(tri_ref[...], adt) for adt in adts]
        for r, dt, adt, pre in zip(rows, dts, adts, prefix):
            tot = pre[L - 1:L, :]
            u = jnp.where(fwd_lane, pre, tot - pre + adt)
            u_scr[r, :] = u * LOG2E
            dtw_scr[r, :] = (dt * jnp.exp(tot - u)).astype(BF16)
            vt_scr[:, r] = ((u - jnp.log(dt)) * LOG2E).T
        return carry

    lax.fori_loop(0, nchunks // SSD_EXP_UNROLL, exponents, 0)
    st_scr[0] = jnp.zeros((2, N, SSD_GW), F32)

    def chunk_rows(c):
        return pl.ds(pl.multiple_of(c * L, L), L)

    def state_inputs(d, c):
        expand = exp_ref[d]
        edge = u_scr[pl.ds(pl.multiple_of(c * L + (0 if d else L - 8), 8), 8), :]
        tot = edge[0:1, :] if d else edge[7:8, :]
        decay_e = _dot_exact_rhs(jnp.exp2(jnp.broadcast_to(tot, (8, LANES))), expand)[0:1, :]
        return decay_e, _dot(dtw_scr[chunk_rows(c), :], expand)

    def gram(c):
        rows = chunk_rows(c)
        return lax.dot_general(cc_scr[rows, :], bc_scr[rows, :], (((1,), (1,)), ((), ())),
                               preferred_element_type=F32)

    def advance_state(d, c, cur, decay_e, dtw_e):
        rows = chunk_rows(c)
        wts = xc_scr[rows, :] * dtw_e.astype(BF16)
        st_scr[1 - cur, d] = st_scr[cur, d] * decay_e + lax.dot_general(
            bc_scr[rows, :], wts, (((0,), (0,)), ((), ())), preferred_element_type=F32)

    def output_operands(slot, d, c, cur, g):
        rows = chunk_rows(c)
        u = u_scr[rows, :]
        v_t = vt_scr[:, rows]
        ccv = cc_scr[rows, :]
        mask = (li >= si) if d == 0 else (si >= li)
        for p in range(SSD_HPG // 2):
            cols = slice(p * LANES, (p + 1) * LANES)
            rhs_scr[slot, d, p, 0:L, :] = xc_scr[rows, cols]
            rhs_scr[slot, d, p, L:2 * L, :] = st_scr[cur, d, :, cols].astype(BF16)
            for hh in range(2):
                col = d * SSD_HPG + 2 * p + hh
                ucol = jnp.broadcast_to(u[:, col:col + 1], (L, L))
                dec = jnp.exp2(jnp.where(mask, ucol - v_t[col:col + 1, :], NEG_BIG))
                lhs_scr[slot, d, p, hh * L:(hh + 1) * L, 0:L] = (g * dec).astype(BF16)
                lhs_scr[slot, d, p, hh * L:(hh + 1) * L, L:2 * L] = ccv * jnp.exp2(ucol).astype(BF16)

    def outputs(slot, chunks):
        for d in range(2):
            rows = chunk_rows(chunks[d])
            parts = []
            for p in range(SSD_HPG // 2):
                y2 = _dot(lhs_scr[slot, d, p], rhs_scr[slot, d, p])
                parts.append(jnp.where(first_head, y2[:L], y2[L:]))
            y_scr[rows, :] += jnp.concatenate(parts, axis=1)

    lhs_scr[1] = jnp.zeros(lhs_scr.shape[1:], BF16)
    rhs_scr[1] = jnp.zeros(rhs_scr.shape[1:], BF16)

    def step(i, cur):
        chunks = (i, nchunks - 1 - i)
        grams = [gram(chunks[d]) for d in range(2)]
        outputs(1 - cur, (jnp.maximum(i - 1, 0), jnp.minimum(nchunks - i, nchunks - 1)))
        ins = [state_inputs(d, chunks[d]) for d in range(2)]
        for d in range(2):
            output_operands(cur, d, chunks[d], cur, grams[d])
        for d in range(2):
            advance_state(d, chunks[d], cur, *ins[d])

    def main(k, carry):
        step(2 * k, 0)
        step(2 * k + 1, 1)
        return carry

    lax.fori_loop(0, nchunks // 2, main, 0)
    outputs((nchunks - 1) % 2, (nchunks - 1, 0))

    def epilogue(c, carry):
        rows = pl.ds(pl.multiple_of(c * L, L), L)
        y = y_scr[rows, :] * _silu(z_ref[0, rows, :].astype(F32))
        ms = jnp.mean(y * y, axis=-1, keepdims=True)
        o_ref[0, rows, :] = (y * lax.rsqrt(ms + RMS_EPS) * nw_ref[0]).astype(BF16)
        return carry

    lax.fori_loop(0, nchunks, epilogue, 0)


def _ssd(proj3, dt3, cw_x, cw_bc, cb_x, cb_bc, dtb, alog, dskip, nw, tri, expand):
    b, s, _ = proj3.shape
    nchunks = s // SSD_CHUNK
    N = SSD_STATE
    seq = lambda width, col: pl.BlockSpec((1, s, width), lambda i, g, col=col, width=width: (i, 0, col // width + g))
    per_group = lambda shape: pl.BlockSpec((1,) + shape, lambda i, g: (g,) + (0,) * len(shape))
    return pl.pallas_call(
        functools.partial(_ssd_body, nchunks=nchunks),
        grid=(b, SSD_GROUPS),
        in_specs=[
            seq(SSD_GW, COL_X), seq(N, COL_B), seq(N, COL_C), seq(SSD_GW, COL_Z),
            pl.BlockSpec((1, s, LANES), lambda i, g: (i, 0, g)),
            per_group((SSD_CONV, SSD_GW)), per_group((SSD_CONV, 2 * N)),
            per_group((1, SSD_GW)), per_group((1, 2 * N)),
            per_group((1, LANES)), per_group((1, LANES)),
            per_group((1, SSD_GW)), per_group((1, SSD_GW)),
            pl.BlockSpec((SSD_CHUNK, SSD_CHUNK), lambda i, g: (0, 0)),
            pl.BlockSpec((2, LANES, SSD_GW), lambda i, g: (0, 0, 0)),
        ],
        out_specs=pl.BlockSpec((1, s, SSD_GW), lambda i, g: (i, 0, g)),
        out_shape=jax.ShapeDtypeStruct((b, s, SSD_WIDTH), BF16),
        scratch_shapes=[
            pltpu.VMEM((s, SSD_GW), BF16),
            pltpu.VMEM((s, N), BF16),
            pltpu.VMEM((s, N), BF16),
            pltpu.VMEM((s, SSD_GW), F32),
            pltpu.VMEM((2, 2, N, SSD_GW), F32),
            pltpu.VMEM((s, LANES), F32),
            pltpu.VMEM((LANES, s), F32),
            pltpu.VMEM((s, LANES), BF16),
            pltpu.VMEM((2, 2, SSD_HPG // 2, 2 * SSD_CHUNK, SSD_CHUNK + N), BF16),
            pltpu.VMEM((2, 2, SSD_HPG // 2, SSD_CHUNK + N, LANES), BF16),
        ],
        compiler_params=pltpu.CompilerParams(
            dimension_semantics=("arbitrary", "arbitrary"),
            vmem_limit_bytes=SSD_VMEM_LIMIT_BYTES),
        name="ssd",
    )(proj3, proj3, proj3, proj3, dt3, cw_x, cw_bc, cb_x, cb_bc, dtb, alog, dskip, nw, tri, expand)


def _ssd_constants():
    tri = np.tril(np.ones((SSD_CHUNK, SSD_CHUNK), np.float32))
    expand = np.zeros((2, LANES, SSD_GW), np.float32)
    for d in range(2):
        for h in range(SSD_HPG):
            expand[d, d * SSD_HPG + h, h * SSD_HEAD_DIM:(h + 1) * SSD_HEAD_DIM] = 1.0
    return jnp.asarray(tri, BF16), jnp.asarray(expand, BF16)


FNET_CW = 256
FNET_N2 = GRID_W
FNET_SUB = 8


def _fnet_body(u_ref, g_ref, wf_ref, cs_ref, k1_ref, e2_ref, o_ref, re_scr, im_scr, out_scr, *, n1):
    n2 = FNET_N2
    s = n1 * n2
    gd = FNET_GROUP_DIM
    ngrp = FNET_CW // gd
    sub = FNET_SUB
    tr = sub * n2

    def load(scr, idx):
        parts = []
        for gi in range(ngrp):
            v = scr.at[gi][idx]
            parts.append(v.reshape(v.shape[0] * v.shape[1], gd))
        return jnp.concatenate(parts, axis=1)

    def store(scr, idx, val, shape):
        for gi in range(ngrp):
            scr.at[gi][idx] = val[:, gi * gd:(gi + 1) * gd].reshape(shape)

    def chan(t, carry):
        rows = pl.ds(pl.multiple_of(t * tr, tr), tr)
        blk = (pl.ds(pl.multiple_of(t * sub, sub), sub), slice(None), slice(None))
        for gi in range(ngrp):
            ri = _dot(u_ref[0, rows, gi * gd:(gi + 1) * gd], cs_ref[...])
            re_scr.at[gi][blk] = ri[:, :gd].reshape(sub, n2, gd)
            im_scr.at[gi][blk] = ri[:, gd:].reshape(sub, n2, gd)
        return carry

    lax.fori_loop(0, n1 // sub, chan, 0)

    def stage1(j, carry):
        idx = (slice(None), pl.ds(pl.multiple_of(j * sub, sub), sub), slice(None))
        rhs = jnp.concatenate([load(re_scr, idx), load(im_scr, idx)], axis=0).astype(BF16)
        t = _dot(k1_ref[...], rhs)
        store(re_scr, idx, t[:sub * n1], (n1, sub, gd))
        store(im_scr, idx, t[sub * n1:], (n1, sub, gd))
        return carry

    lax.fori_loop(0, n2 // sub, stage1, 0)

    def stage2(j, carry):
        src = (pl.ds(pl.multiple_of(j * sub, sub), sub), slice(None), slice(None))
        dst = (slice(None), pl.ds(pl.multiple_of(j * sub, sub), sub), slice(None))
        rhs = jnp.concatenate([load(re_scr, src), load(im_scr, src)], axis=0).astype(BF16)
        store(out_scr, dst, _dot(e2_ref[j], rhs), (n2, sub, gd))
        return carry

    lax.fori_loop(0, n1 // sub, stage2, 0)

    ortho = 1.0 / math.sqrt(s * gd)

    tm = sub * n1

    def mix(t, carry):
        rows = pl.ds(pl.multiple_of(t * tm, tm), tm)
        blk = (pl.ds(pl.multiple_of(t * sub, sub), sub), slice(None), slice(None))
        outs = [_dot(out_scr.at[gi][blk].reshape(tm, gd).astype(BF16), wf_ref[gi]) for gi in range(ngrp)]
        y = jnp.concatenate(outs, axis=1) * ortho
        o_ref[0, rows, :] = (y * _silu(g_ref[0, rows, :].astype(F32))).astype(BF16)
        return carry

    lax.fori_loop(0, n2 // sub, mix, 0)


def _fnet(proj3, w_f, cs_ch, k1m, e2):
    b, s, _ = proj3.shape
    n2 = FNET_N2
    n1 = s // n2
    nblk = FNET_WIDTH // FNET_CW
    ngrp = FNET_CW // FNET_GROUP_DIM
    const = lambda shape: pl.BlockSpec(shape, lambda i, j: (0,) * len(shape))
    return pl.pallas_call(
        functools.partial(_fnet_body, n1=n1),
        grid=(b, nblk),
        in_specs=[
            pl.BlockSpec((1, s, FNET_CW), lambda i, j: (i, 0, COL_F // FNET_CW + j)),
            pl.BlockSpec((1, s, FNET_CW), lambda i, j: (i, 0, COL_GF // FNET_CW + j)),
            pl.BlockSpec((ngrp, FNET_GROUP_DIM, FNET_GROUP_DIM), lambda i, j: (j, 0, 0)),
            const(cs_ch.shape), const(k1m.shape), const(e2.shape),
        ],
        out_specs=pl.BlockSpec((1, s, FNET_CW), lambda i, j: (i, 0, j)),
        out_shape=jax.ShapeDtypeStruct((b, s, FNET_WIDTH), BF16),
        scratch_shapes=[pltpu.VMEM((ngrp, n1, n2, FNET_GROUP_DIM), F32),
                        pltpu.VMEM((ngrp, n1, n2, FNET_GROUP_DIM), F32),
                        pltpu.VMEM((ngrp, n2, n1, FNET_GROUP_DIM), F32)],
        compiler_params=pltpu.CompilerParams(
            dimension_semantics=("arbitrary", "arbitrary"),
            vmem_limit_bytes=VMEM_LIMIT_BYTES),
        name="fnet",
    )(proj3, proj3, w_f, cs_ch, k1m, e2)


def _fnet_constants(s):
    n2 = FNET_N2
    n1 = s // n2
    gd = FNET_GROUP_DIM
    sub = FNET_SUB
    jk = np.outer(np.arange(gd), np.arange(gd)) * (2.0 * np.pi / gd)
    cs_ch = np.concatenate([np.cos(jk), -np.sin(jk)], axis=1)
    rk = np.outer(np.arange(n1), np.arange(n1)) * (2.0 * np.pi / n1)
    eye = np.eye(sub)
    cr, sr = np.kron(np.cos(rk), eye), np.kron(np.sin(rk), eye)
    k1m = np.block([[cr, sr], [-sr, cr]])
    j = np.arange(n1 // sub)[:, None, None, None, None]
    k2 = np.arange(n2)[None, :, None, None, None]
    a = np.arange(sub)[None, None, :, None, None]
    a2 = np.arange(sub)[None, None, None, :, None]
    c = np.arange(n2)[None, None, None, None, :]
    theta = (2.0 * np.pi / s) * ((c * (sub * j + a + n1 * k2)) % s)
    same = (a == a2)
    e2 = np.concatenate([(np.cos(theta) * same).reshape(n1 // sub, n2 * sub, sub * n2),
                         (np.sin(theta) * same).reshape(n1 // sub, n2 * sub, sub * n2)], axis=2)
    return (jnp.asarray(cs_ch, BF16), jnp.asarray(k1m, BF16), jnp.asarray(e2, BF16))


OUTPROJ_TM = 1024


def _outproj_body(x_ref, na_ref, ssd_ref, fn_ref, w_ref, fw_ref, o_ref, *, final):
    acc = _dot(na_ref[...], w_ref[0:NA_WIDTH, :])
    acc = acc + _dot(ssd_ref[...], w_ref[NA_WIDTH:NA_WIDTH + SSD_WIDTH, :])
    acc = acc + _dot(fn_ref[...], w_ref[NA_WIDTH + SSD_WIDTH:D_MIX, :])
    y = x_ref[...] + acc
    if final:
        ms = jnp.mean(y * y, axis=-1, keepdims=True)
        y = y * lax.rsqrt(ms + RMS_EPS) * fw_ref[...]
    o_ref[...] = y


def _outproj(x2, na2, ssd2, fn2, w_out, final_w, final):
    m = x2.shape[0]
    tm = min(OUTPROJ_TM, m)
    row = lambda width: pl.BlockSpec((tm, width), lambda i: (i, 0))
    return pl.pallas_call(
        functools.partial(_outproj_body, final=final),
        grid=(m // tm,),
        in_specs=[
            row(D_MODEL), row(NA_WIDTH), row(SSD_WIDTH), row(FNET_WIDTH),
            pl.BlockSpec((D_MIX, D_MODEL), lambda i: (0, 0)),
            pl.BlockSpec((1, D_MODEL), lambda i: (0, 0)),
        ],
        out_specs=row(D_MODEL),
        out_shape=jax.ShapeDtypeStruct((m, D_MODEL), F32),
        compiler_params=pltpu.CompilerParams(
            dimension_semantics=("arbitrary",),
            vmem_limit_bytes=VMEM_LIMIT_BYTES),
        name="outproj_final" if final else "outproj",
    )(x2, na2, ssd2, fn2, w_out, final_w.reshape(1, D_MODEL))


def _layer_params(w_in, conv_w, conv_b, dt_bias, a_log, d_skip, ssd_norm_w):
    dt0 = 4 * NA_WIDTH + SSD_WIDTH + (SSD_WIDTH + 2 * SSD_GROUPS * SSD_STATE)
    w_main = jnp.concatenate([w_in[:, :dt0], w_in[:, dt0 + 2 * SSD_HEADS:]], axis=1).astype(BF16)
    wdt = w_in[:, dt0:dt0 + 2 * SSD_HEADS].reshape(D_MODEL, 2, SSD_GROUPS, SSD_HPG)
    wdt = wdt.transpose(0, 2, 1, 3).reshape(D_MODEL, SSD_GROUPS, 2 * SSD_HPG)
    wdt = jnp.pad(wdt, ((0, 0), (0, 0), (0, LANES - 2 * SSD_HPG))).reshape(D_MODEL, DT_COLS).astype(BF16)

    def head_rows(v):
        v = v.reshape(2, SSD_GROUPS, SSD_HPG).transpose(1, 0, 2).reshape(SSD_GROUPS, 1, 2 * SSD_HPG)
        return jnp.pad(v, ((0, 0), (0, 0), (0, LANES - 2 * SSD_HPG)))

    N = SSD_STATE
    cw_x = conv_w[:, :SSD_WIDTH].reshape(SSD_CONV, SSD_GROUPS, SSD_GW).transpose(1, 0, 2)
    cb_x = conv_b[:SSD_WIDTH].reshape(SSD_GROUPS, 1, SSD_GW)
    cw_b = conv_w[:, SSD_WIDTH:SSD_WIDTH + SSD_GROUPS * N].reshape(SSD_CONV, SSD_GROUPS, N)
    cw_c = conv_w[:, SSD_WIDTH + SSD_GROUPS * N:].reshape(SSD_CONV, SSD_GROUPS, N)
    cw_bc = jnp.concatenate([cw_b, cw_c], axis=2).transpose(1, 0, 2)
    cb_b = conv_b[SSD_WIDTH:SSD_WIDTH + SSD_GROUPS * N].reshape(SSD_GROUPS, 1, N)
    cb_c = conv_b[SSD_WIDTH + SSD_GROUPS * N:].reshape(SSD_GROUPS, 1, N)
    cb_bc = jnp.concatenate([cb_b, cb_c], axis=2)
    dskip = jnp.repeat(d_skip, SSD_HEAD_DIM).reshape(SSD_GROUPS, 1, SSD_GW)
    nw = ssd_norm_w.reshape(SSD_GROUPS, 1, SSD_GW)
    return dict(w_main=w_main, w_dt=wdt, cw_x=cw_x, cw_bc=cw_bc, cb_x=cb_x, cb_bc=cb_bc,
                dtb=head_rows(dt_bias), alog=head_rows(a_log), dskip=dskip, nw=nw)


def kernel(x, norm_w, w_in, na_rpb, conv_w, conv_b, dt_bias, a_log, d_skip, ssd_norm_w,
           w_fourier, w_out, final_norm_w):
    b, s, d = x.shape
    depth = w_in.shape[0]
    tri, expand = _ssd_constants()
    cs_ch, r1, e2 = _fnet_constants(s)
    x2 = x.reshape(b * s, d)
    for i in range(depth):
        p = _layer_params(w_in[i], conv_w[i], conv_b[i], dt_bias[i], a_log[i], d_skip[i], ssd_norm_w[i])
        proj, dt = _inproj(x2, norm_w[i], p["w_main"], p["w_dt"])
        proj3 = proj.reshape(b, s, PROJ_COLS)
        dt3 = dt.reshape(b, s, DT_COLS)
        na = _natten(proj3, _natten_bias_table(na_rpb[i]))
        ssd = _ssd(proj3, dt3, p["cw_x"], p["cw_bc"], p["cb_x"], p["cb_bc"], p["dtb"], p["alog"],
                   p["dskip"], p["nw"], tri, expand)
        fn = _fnet(proj3, w_fourier[i].astype(BF16), cs_ch, r1, e2)
        x2 = _outproj(x2, na.reshape(b * s, NA_WIDTH), ssd.reshape(b * s, SSD_WIDTH),
                      fn.reshape(b * s, FNET_WIDTH), w_out[i].astype(BF16), final_norm_w,
                      final=(i == depth - 1))
    return x2.reshape(b, s, d)
```

```python
import functools
import math

import numpy as np
import jax
import jax.numpy as jnp
from jax import lax
from jax.experimental import pallas as pl
from jax.experimental.pallas import tpu as pltpu

F32 = jnp.float32
BF16 = jnp.bfloat16

D_MODEL = 1024
GRID_W = 64
NA_HEADS = 8
NA_HEAD_DIM = 64
NA_WIDTH = NA_HEADS * NA_HEAD_DIM
NA_WIN_H = 8
NA_WIN_W = 16
SSD_HEADS = 16
SSD_HEAD_DIM = 64
SSD_WIDTH = SSD_HEADS * SSD_HEAD_DIM
SSD_GROUPS = 2
SSD_HPG = SSD_HEADS // SSD_GROUPS
SSD_GW = SSD_HPG * SSD_HEAD_DIM
SSD_STATE = 128
SSD_CONV = 5
SSD_CHUNK = 128
FNET_GROUPS = 4
FNET_GROUP_DIM = 128
FNET_WIDTH = FNET_GROUPS * FNET_GROUP_DIM
D_MIX = NA_WIDTH + SSD_WIDTH + FNET_WIDTH
RMS_EPS = 1e-6

LANES = 128
VMEM_LIMIT_BYTES = 56 * 1024 * 1024
SSD_VMEM_LIMIT_BYTES = 60 * 1024 * 1024

PROJ_COLS = 4 * NA_WIDTH + SSD_WIDTH + (SSD_WIDTH + 2 * SSD_GROUPS * SSD_STATE) + 2 * FNET_WIDTH
COL_Q, COL_K, COL_V, COL_GNA = 0, 512, 1024, 1536
COL_Z = 2048
COL_X = 3072
COL_B = 4096
COL_C = 4352
COL_F = 4608
COL_GF = 5120
DT_COLS = SSD_GROUPS * LANES

NEG_BIG = -1e30
LOG2E = math.log2(math.e)


def _silu(v):
    return v * (1.0 / (1.0 + jnp.exp(-v)))


def _softplus(v):
    return jnp.maximum(v, 0.0) + jnp.log(1.0 + jnp.exp(-jnp.abs(v)))


def _split3(v):
    hi = v.astype(BF16)
    r1 = v - hi.astype(F32)
    mid = r1.astype(BF16)
    lo = (r1 - mid.astype(F32)).astype(BF16)
    return hi, mid, lo


def _dot(a, b):
    return jnp.dot(a, b, preferred_element_type=F32)


def _dot_exact_rhs(a_f32, b_bf16):
    hi, mid, lo = _split3(a_f32)
    return _dot(hi, b_bf16) + _dot(mid, b_bf16) + _dot(lo, b_bf16)


def _dot_exact_lhs(a_bf16, b_f32):
    hi, mid, lo = _split3(b_f32)
    return _dot(a_bf16, hi) + _dot(a_bf16, mid) + _dot(a_bf16, lo)


INPROJ_TM = 512
INPROJ_HALO = 8
INPROJ_CONV_ROWS = 64
INPROJ_CONV_COLS = 256
INPROJ_PLAIN_CHUNKS = tuple((c, c + 512) for c in (*range(0, COL_X, 512), *range(COL_F, PROJ_COLS, 512)))
INPROJ_CONV_CHUNKS = ((COL_X, 3584), (3584, COL_B), (COL_B, COL_F))


def _inproj_body(x_ref, xprev_ref, xnext_ref, nw_ref, w_ref, wdt_ref, cw_ref, cb_ref, proj_ref, dt_ref,
                 h_scr, win_scr, *, tiles_per_seq):
    tm = x_ref.shape[0]

    def normed(x):
        ms = jnp.mean(x * x, axis=-1, keepdims=True)
        return (x * lax.rsqrt(ms + RMS_EPS) * nw_ref[...]).astype(BF16)

    h_scr[...] = normed(x_ref[...])
    tile = pl.program_id(0) % tiles_per_seq
    h_halo = normed(jnp.concatenate([xprev_ref[...], xnext_ref[...]], axis=0))
    pad = SSD_CONV // 2
    halo_rows = INPROJ_HALO
    blk = INPROJ_CONV_ROWS
    n = blk + 2 * halo_rows
    def plain(lo, hi):
        proj_ref[:, lo:hi] = _dot(h_scr[...], w_ref[:, lo:hi]).astype(BF16)

    def conv(k, lo, hi):
        halo = _dot(h_halo, w_ref[:, lo:hi])
        win_scr[k, 0:halo_rows, :] = jnp.where(tile > 0, halo[:halo_rows], 0.0)
        win_scr[k, halo_rows:halo_rows + tm, :] = _dot(h_scr[...], w_ref[:, lo:hi])
        win_scr[k, halo_rows + tm:, :] = jnp.where(tile < tiles_per_seq - 1, halo[halo_rows:], 0.0)
        for r in range(tm // blk):
            for c0 in range(0, hi - lo, INPROJ_CONV_COLS):
                c1 = c0 + INPROJ_CONV_COLS
                wcols = slice(lo - COL_X + c0, lo - COL_X + c1)
                win = win_scr[k, r * blk:r * blk + n, c0:c1]
                acc = cb_ref[:, wcols]
                for j in range(SSD_CONV):
                    shift = (pad - j) % n
                    shifted = win if shift == 0 else pltpu.roll(win, shift, 0)
                    acc = acc + shifted[halo_rows:halo_rows + blk] * cw_ref[j:j + 1, wcols]
                proj_ref[r * blk:(r + 1) * blk, lo + c0:lo + c1] = _silu(acc).astype(BF16)

    plain_chunks = list(INPROJ_PLAIN_CHUNKS)
    per_conv = len(plain_chunks) // len(INPROJ_CONV_CHUNKS)
    for k, (lo, hi) in enumerate(INPROJ_CONV_CHUNKS):
        conv(k, lo, hi)
        for _ in range(per_conv):
            plain(*plain_chunks.pop(0))
    dt_ref[...] = _dot(h_scr[...], wdt_ref[...])
    for lo, hi in plain_chunks:
        plain(lo, hi)


def _inproj(x2, seq_len, norm_w, w_main, w_dt, conv_w, conv_b):
    m = x2.shape[0]
    tm = min(INPROJ_TM, seq_len)
    halo_blocks = tm // INPROJ_HALO
    last_halo_block = m // INPROJ_HALO - 1
    conv_cols = INPROJ_CONV_CHUNKS[0][1] - INPROJ_CONV_CHUNKS[0][0]
    assert all(hi - lo == conv_cols for lo, hi in INPROJ_CONV_CHUNKS) and tm % INPROJ_CONV_ROWS == 0
    return pl.pallas_call(
        functools.partial(_inproj_body, tiles_per_seq=seq_len // tm),
        grid=(m // tm,),
        in_specs=[
            pl.BlockSpec((tm, D_MODEL), lambda i: (i, 0)),
            pl.BlockSpec((INPROJ_HALO, D_MODEL), lambda i: (jnp.maximum(i * halo_blocks - 1, 0), 0)),
            pl.BlockSpec((INPROJ_HALO, D_MODEL),
                         lambda i: (jnp.minimum((i + 1) * halo_blocks, last_halo_block), 0)),
            pl.BlockSpec((1, D_MODEL), lambda i: (0, 0)),
            pl.BlockSpec((D_MODEL, PROJ_COLS), lambda i: (0, 0)),
            pl.BlockSpec((D_MODEL, DT_COLS), lambda i: (0, 0)),
            pl.BlockSpec((SSD_CONV, COL_F - COL_X), lambda i: (0, 0)),
            pl.BlockSpec((1, COL_F - COL_X), lambda i: (0, 0)),
        ],
        out_specs=[
            pl.BlockSpec((tm, PROJ_COLS), lambda i: (i, 0)),
            pl.BlockSpec((tm, DT_COLS), lambda i: (i, 0)),
        ],
        out_shape=[
            jax.ShapeDtypeStruct((m, PROJ_COLS), BF16),
            jax.ShapeDtypeStruct((m, DT_COLS), F32),
        ],
        scratch_shapes=[pltpu.VMEM((tm, D_MODEL), BF16),
                        pltpu.VMEM((len(INPROJ_CONV_CHUNKS), tm + 2 * INPROJ_HALO, conv_cols), F32)],
        compiler_params=pltpu.CompilerParams(
            dimension_semantics=("arbitrary",),
            vmem_limit_bytes=VMEM_LIMIT_BYTES),
        name="inproj",
    )(x2, x2, x2, norm_w.reshape(1, D_MODEL), w_main, w_dt, conv_w, conv_b.reshape(1, COL_F - COL_X))


NA_ROWS_PER_STEP = 8
NA_ROWS_PER_ITER = 2
NA_PAIRS = NA_HEADS // 2
NA_BAND = NA_WIN_H * GRID_W


def _natten_body(q_ref, k_ref, v_ref, g_ref, bias_ref, o_ref, *, rows):
    rb = pl.program_id(1)
    lane = lax.broadcasted_iota(jnp.int32, (GRID_W, LANES), 1)
    first_head = lane < NA_HEAD_DIM
    scale = NA_HEAD_DIM ** -0.5

    def row_geometry(rr):
        r = rb * NA_ROWS_PER_STEP + rr
        r0 = jnp.clip(r - NA_WIN_H // 2, 0, rows - NA_WIN_H)
        return (r - r0, pl.multiple_of(r0 * GRID_W, GRID_W), pl.multiple_of(rr * GRID_W, GRID_W))

    def scores(geom):
        delta, kstart, qstart = geom
        logits = []
        for p in range(NA_PAIRS):
            cols = slice(p * LANES, (p + 1) * LANES)
            q2 = q_ref[0, pl.ds(qstart, GRID_W), cols] * scale
            zero = jnp.zeros_like(q2)
            qs = jnp.concatenate([jnp.where(first_head, q2, zero),
                                  jnp.where(first_head, zero, q2)], axis=0)
            kp = k_ref[0, pl.ds(kstart, NA_BAND), cols]
            lg = lax.dot_general(qs, kp, (((1,), (1,)), ((), ())),
                                 preferred_element_type=F32)
            logits.append(lg + bias_ref[delta, p])
        return logits

    def softmax_parts(logits):
        probs, dens = [], []
        for lg in logits:
            mx = jnp.max(lg, axis=-1, keepdims=True)
            e = jnp.exp(lg - mx)
            dens.append(jnp.sum(e, axis=-1, keepdims=True))
            probs.append(e.astype(BF16))
        return probs, dens

    def values(geom, probs, dens):
        _, kstart, qstart = geom
        outs = []
        for p in range(NA_PAIRS):
            vp = v_ref[0, pl.ds(kstart, NA_BAND), p * LANES:(p + 1) * LANES]
            o = _dot(probs[p], vp) / dens[p]
            outs.append(jnp.where(first_head, o[:GRID_W], o[GRID_W:]))
        att = jnp.concatenate(outs, axis=1)
        gate = g_ref[0, pl.ds(qstart, GRID_W), :].astype(F32)
        o_ref[0, pl.ds(qstart, GRID_W), :] = (att * _silu(gate)).astype(BF16)

    def row_group(i, carry):
        geoms = [row_geometry(i * NA_ROWS_PER_ITER + j) for j in range(NA_ROWS_PER_ITER)]
        logits = [scores(g) for g in geoms]
        for g, lg in zip(geoms, logits):
            values(g, *softmax_parts(lg))
        return carry

    lax.fori_loop(0, NA_ROWS_PER_STEP // NA_ROWS_PER_ITER, row_group, 0)


def _natten(proj3, bias_tab):
    b, s, _ = proj3.shape
    rows = s // GRID_W
    tq = NA_ROWS_PER_STEP * GRID_W
    blk = NA_WIDTH // 512
    return pl.pallas_call(
        functools.partial(_natten_body, rows=rows),
        grid=(b, rows // NA_ROWS_PER_STEP),
        in_specs=[
            pl.BlockSpec((1, tq, NA_WIDTH), lambda i, j: (i, j, COL_Q // NA_WIDTH)),
            pl.BlockSpec((1, s, NA_WIDTH), lambda i, j: (i, 0, COL_K // NA_WIDTH)),
            pl.BlockSpec((1, s, NA_WIDTH), lambda i, j: (i, 0, COL_V // NA_WIDTH)),
            pl.BlockSpec((1, tq, NA_WIDTH), lambda i, j: (i, j, COL_GNA // NA_WIDTH)),
            pl.BlockSpec((NA_WIN_H, NA_PAIRS, 2 * GRID_W, NA_BAND), lambda i, j: (0, 0, 0, 0)),
        ],
        out_specs=pl.BlockSpec((1, tq, NA_WIDTH), lambda i, j: (i, j, 0)),
        out_shape=jax.ShapeDtypeStruct((b, s, NA_WIDTH), BF16),
        compiler_params=pltpu.CompilerParams(
            dimension_semantics=("arbitrary", "arbitrary"),
            vmem_limit_bytes=VMEM_LIMIT_BYTES),
        name="natten",
    )(proj3, proj3, proj3, proj3, bias_tab)


def _natten_bias_table(rpb):
    qc = np.arange(GRID_W)[:, None]
    kc = np.arange(GRID_W)[None, :]
    col_start = np.clip(qc - NA_WIN_W // 2, 0, GRID_W - NA_WIN_W)
    inside = (kc >= col_start) & (kc < col_start + NA_WIN_W)
    col_rel = np.clip(kc - qc + (NA_WIN_W - 1), 0, 2 * NA_WIN_W - 2)
    onehot = (np.arange(2 * NA_WIN_W - 1)[None, None, :] == col_rel[:, :, None]) & inside[:, :, None]
    toe = jnp.einsum("hrj,qkj->hrqk", rpb.astype(F32), jnp.asarray(onehot, F32),
                     precision=lax.Precision.HIGHEST)
    toe = toe + jnp.asarray(np.where(inside, 0.0, NEG_BIG), F32)
    tab = jnp.stack([toe[:, NA_WIN_H - 1 - dl:2 * NA_WIN_H - 1 - dl] for dl in range(NA_WIN_H)])
    tab = tab.transpose(0, 1, 3, 2, 4)
    return tab.reshape(NA_WIN_H, NA_PAIRS, 2 * GRID_W, NA_BAND)


SSD_EXP_UNROLL = 4


def _ssd_body(x_ref, b_ref, c_ref, z_ref, dt_ref, dtb_ref, alog_ref, dskip_ref, nw_ref, tri_ref, exp_ref,
              o_ref, y_scr, st_scr, u_scr, vt_scr, dtw_scr, lhs_scr, rhs_scr, *, nchunks):
    L = SSD_CHUNK
    N = SSD_STATE

    def chunk_rows(c):
        return pl.ds(pl.multiple_of(c * L, L), L)

    a_row = -jnp.exp(alog_ref[0])
    dtb_row = dtb_ref[0]
    li = lax.broadcasted_iota(jnp.int32, (L, L), 0)
    si = lax.broadcasted_iota(jnp.int32, (L, L), 1)
    lane = lax.broadcasted_iota(jnp.int32, (L, LANES), 1)
    first_head = lane < SSD_HEAD_DIM
    fwd_lane = lane < SSD_HPG

    def exponents(i, carry):
        group = [i * SSD_EXP_UNROLL + j for j in range(SSD_EXP_UNROLL)]
        rows = [pl.ds(pl.multiple_of(c * L, L), L) for c in group]
        dts = [_softplus(dt_ref[0, r, :] + dtb_row) for r in rows]
        adts = [dt * a_row for dt in dts]
        prefix = [_dot_exact_lhs(tri_ref[...], adt) for adt in adts]
        for r, dt, adt, pre in zip(rows, dts, adts, prefix):
            tot = pre[L - 1:L, :]
            u = jnp.where(fwd_lane, pre, tot - pre + adt)
            u_scr[r, :] = u * LOG2E
            dtw_scr[r, :] = (dt * jnp.exp(tot - u)).astype(BF16)
            vt_scr[:, r] = ((u - jnp.log(dt)) * LOG2E).T
            y_scr[r, :] = x_ref[0, r, :].astype(F32) * dskip_ref[0]
        return carry

    lax.fori_loop(0, nchunks // SSD_EXP_UNROLL, exponents, 0)
    st_scr[0] = jnp.zeros((2, N, SSD_GW), F32)

    def state_inputs(d, c):
        expand = exp_ref[d]
        edge = u_scr[pl.ds(pl.multiple_of(c * L + (0 if d else L - 8), 8), 8), :]
        tot = edge[0:1, :] if d else edge[7:8, :]
        decay_e = _dot_exact_rhs(jnp.exp2(jnp.broadcast_to(tot, (8, LANES))), expand)[0:1, :]
        return decay_e, _dot(dtw_scr[chunk_rows(c), :], expand)

    def gram(c):
        rows = chunk_rows(c)
        return lax.dot_general(c_ref[0, rows, :], b_ref[0, rows, :], (((1,), (1,)), ((), ())),
                               preferred_element_type=F32)

    def advance_state(d, c, cur, decay_e, dtw_e):
        rows = chunk_rows(c)
        wts = x_ref[0, rows, :] * dtw_e.astype(BF16)
        st_scr[1 - cur, d] = st_scr[cur, d] * decay_e + lax.dot_general(
            b_ref[0, rows, :], wts, (((0,), (0,)), ((), ())), preferred_element_type=F32)

    def output_operands(slot, d, c, cur, g):
        rows = chunk_rows(c)
        u = u_scr[rows, :]
        v_t = vt_scr[:, rows]
        ccv = c_ref[0, rows, :]
        mask = (li >= si) if d == 0 else (si >= li)
        for p in range(SSD_HPG // 2):
            cols = slice(p * LANES, (p + 1) * LANES)
            rhs_scr[slot, p, 0:L, :] = x_ref[0, rows, cols]
            rhs_scr[slot, p, L:2 * L, :] = st_scr[cur, d, :, cols].astype(BF16)
            for hh in range(2):
                col = d * SSD_HPG + 2 * p + hh
                ucol = jnp.broadcast_to(u[:, col:col + 1], (L, L))
                dec = jnp.exp2(jnp.where(mask, ucol - v_t[col:col + 1, :], NEG_BIG))
                lhs_scr[slot, p, hh * L:(hh + 1) * L, 0:L] = (g * dec).astype(BF16)
                lhs_scr[slot, p, hh * L:(hh + 1) * L, L:2 * L] = ccv * jnp.exp2(ucol).astype(BF16)

    def outputs(slot, c):
        rows = chunk_rows(c)
        parts = []
        for p in range(SSD_HPG // 2):
            y2 = _dot(lhs_scr[slot, p], rhs_scr[slot, p])
            parts.append(jnp.where(first_head, y2[:L], y2[L:]))
        y_scr[rows, :] += jnp.concatenate(parts, axis=1)

    def clear_operands(slot):
        lhs_scr[slot] = jnp.zeros(lhs_scr.shape[1:], BF16)
        rhs_scr[slot] = jnp.zeros(rhs_scr.shape[1:], BF16)

    def finish_chunk(c):
        rows = chunk_rows(c)
        y = y_scr[rows, :] * _silu(z_ref[0, rows, :].astype(F32))
        ms = jnp.mean(y * y, axis=-1, keepdims=True)
        o_ref[0, rows, :] = (y * lax.rsqrt(ms + RMS_EPS) * nw_ref[0]).astype(BF16)

    def sweep_step(d, c, prev_c, cur):
        g = gram(c)
        if prev_c is not None:
            outputs(1 - cur, prev_c)
        ins = state_inputs(d, c)
        output_operands(cur, d, c, cur, g)
        advance_state(d, c, cur, *ins)

    last = nchunks - 1
    clear_operands(1)

    def forward(k, carry):
        for cur in range(2):
            i = 2 * k + cur
            sweep_step(0, i, jnp.maximum(i - 1, 0), cur)
        return carry

    lax.fori_loop(0, nchunks // 2, forward, 0)
    outputs(last % 2, last)

    def backward_step(i, cur, finish, first=False):
        c = last - i
        sweep_step(1, c, None if first else c + 1, cur)
        if finish:
            finish_chunk(c + 2)

    backward_step(0, 0, False, first=True)
    backward_step(1, 1, False)

    def backward(k, carry):
        for cur in range(2):
            backward_step(2 * k + cur, cur, True)
        return carry

    lax.fori_loop(1, nchunks // 2, backward, 0)
    outputs(last % 2, 0)
    finish_chunk(1)
    finish_chunk(0)


def _ssd(proj3, dt3, dtb, alog, dskip, nw, tri, expand):
    b, s, _ = proj3.shape
    nchunks = s // SSD_CHUNK
    N = SSD_STATE
    seq = lambda width, col: pl.BlockSpec((1, s, width), lambda i, g, col=col, width=width: (i, 0, col // width + g))
    per_group = lambda shape: pl.BlockSpec((1,) + shape, lambda i, g: (g,) + (0,) * len(shape))
    return pl.pallas_call(
        functools.partial(_ssd_body, nchunks=nchunks),
        grid=(b, SSD_GROUPS),
        in_specs=[
            seq(SSD_GW, COL_X), seq(N, COL_B), seq(N, COL_C), seq(SSD_GW, COL_Z),
            pl.BlockSpec((1, s, LANES), lambda i, g: (i, 0, g)),
            per_group((1, LANES)), per_group((1, LANES)),
            per_group((1, SSD_GW)), per_group((1, SSD_GW)),
            pl.BlockSpec((SSD_CHUNK, SSD_CHUNK), lambda i, g: (0, 0)),
            pl.BlockSpec((2, LANES, SSD_GW), lambda i, g: (0, 0, 0)),
        ],
        out_specs=pl.BlockSpec((1, s, SSD_GW), lambda i, g: (i, 0, g)),
        out_shape=jax.ShapeDtypeStruct((b, s, SSD_WIDTH), BF16),
        scratch_shapes=[
            pltpu.VMEM((s, SSD_GW), F32),
            pltpu.VMEM((2, 2, N, SSD_GW), F32),
            pltpu.VMEM((s, LANES), F32),
            pltpu.VMEM((LANES, s), F32),
            pltpu.VMEM((s, LANES), BF16),
            pltpu.VMEM((2, SSD_HPG // 2, 2 * SSD_CHUNK, SSD_CHUNK + N), BF16),
            pltpu.VMEM((2, SSD_HPG // 2, SSD_CHUNK + N, LANES), BF16),
        ],
        compiler_params=pltpu.CompilerParams(
            dimension_semantics=("arbitrary", "arbitrary"),
            vmem_limit_bytes=SSD_VMEM_LIMIT_BYTES),
        name="ssd",
    )(proj3, proj3, proj3, proj3, dt3, dtb, alog, dskip, nw, tri, expand)


def _ssd_constants():
    tri = np.tril(np.ones((SSD_CHUNK, SSD_CHUNK), np.float32))
    expand = np.zeros((2, LANES, SSD_GW), np.float32)
    for d in range(2):
        for h in range(SSD_HPG):
            expand[d, d * SSD_HPG + h, h * SSD_HEAD_DIM:(h + 1) * SSD_HEAD_DIM] = 1.0
    return jnp.asarray(tri, BF16), jnp.asarray(expand, BF16)


FNET_CW = 256
FNET_N2 = GRID_W
FNET_SUB = 8


def _fnet_body(u_ref, g_ref, wf_ref, cs_ref, k1_ref, e2_ref, o_ref, re_scr, im_scr, out_scr, *, n1):
    n2 = FNET_N2
    s = n1 * n2
    gd = FNET_GROUP_DIM
    ngrp = FNET_CW // gd
    sub = FNET_SUB
    tr = sub * n2

    def load(scr, idx):
        parts = []
        for gi in range(ngrp):
            v = scr.at[gi][idx]
            parts.append(v.reshape(v.shape[0] * v.shape[1], gd))
        return jnp.concatenate(parts, axis=1)

    def store(scr, idx, val, shape):
        for gi in range(ngrp):
            scr.at[gi][idx] = val[:, gi * gd:(gi + 1) * gd].reshape(shape)

    def chan(t, carry):
        rows = pl.ds(pl.multiple_of(t * tr, tr), tr)
        blk = (pl.ds(pl.multiple_of(t * sub, sub), sub), slice(None), slice(None))
        for gi in range(ngrp):
            ri = _dot(u_ref[0, rows, gi * gd:(gi + 1) * gd], cs_ref[...])
            re_scr.at[gi][blk] = ri[:, :gd].reshape(sub, n2, gd)
            im_scr.at[gi][blk] = ri[:, gd:].reshape(sub, n2, gd)
        return carry

    lax.fori_loop(0, n1 // sub, chan, 0)

    def stage1(j, carry):
        idx = (slice(None), pl.ds(pl.multiple_of(j * sub, sub), sub), slice(None))
        rhs = jnp.concatenate([load(re_scr, idx), load(im_scr, idx)], axis=0).astype(BF16)
        t = _dot(k1_ref[...], rhs)
        store(re_scr, idx, t[:sub * n1], (n1, sub, gd))
        store(im_scr, idx, t[sub * n1:], (n1, sub, gd))
        return carry

    lax.fori_loop(0, n2 // sub, stage1, 0)

    def stage2(j, carry):
        src = (pl.ds(pl.multiple_of(j * sub, sub), sub), slice(None), slice(None))
        dst = (slice(None), pl.ds(pl.multiple_of(j * sub, sub), sub), slice(None))
        rhs = jnp.concatenate([load(re_scr, src), load(im_scr, src)], axis=0).astype(BF16)
        store(out_scr, dst, _dot(e2_ref[j], rhs), (n2, sub, gd))
        return carry

    lax.fori_loop(0, n1 // sub, stage2, 0)

    ortho = 1.0 / math.sqrt(s * gd)

    tm = sub * n1

    def mix(t, carry):
        rows = pl.ds(pl.multiple_of(t * tm, tm), tm)
        blk = (pl.ds(pl.multiple_of(t * sub, sub), sub), slice(None), slice(None))
        outs = [_dot(out_scr.at[gi][blk].reshape(tm, gd).astype(BF16), wf_ref[gi]) for gi in range(ngrp)]
        y = jnp.concatenate(outs, axis=1) * ortho
        o_ref[0, rows, :] = (y * _silu(g_ref[0, rows, :].astype(F32))).astype(BF16)
        return carry

    lax.fori_loop(0, n2 // sub, mix, 0)


def _fnet(proj3, w_f, cs_ch, k1m, e2):
    b, s, _ = proj3.shape
    n2 = FNET_N2
    n1 = s // n2
    nblk = FNET_WIDTH // FNET_CW
    ngrp = FNET_CW // FNET_GROUP_DIM
    const = lambda shape: pl.BlockSpec(shape, lambda i, j: (0,) * len(shape))
    return pl.pallas_call(
        functools.partial(_fnet_body, n1=n1),
        grid=(b, nblk),
        in_specs=[
            pl.BlockSpec((1, s, FNET_CW), lambda i, j: (i, 0, COL_F // FNET_CW + j)),
            pl.BlockSpec((1, s, FNET_CW), lambda i, j: (i, 0, COL_GF // FNET_CW + j)),
            pl.BlockSpec((ngrp, FNET_GROUP_DIM, FNET_GROUP_DIM), lambda i, j: (j, 0, 0)),
            const(cs_ch.shape), const(k1m.shape), const(e2.shape),
        ],
        out_specs=pl.BlockSpec((1, s, FNET_CW), lambda i, j: (i, 0, j)),
        out_shape=jax.ShapeDtypeStruct((b, s, FNET_WIDTH), BF16),
        scratch_shapes=[pltpu.VMEM((ngrp, n1, n2, FNET_GROUP_DIM), F32),
                        pltpu.VMEM((ngrp, n1, n2, FNET_GROUP_DIM), F32),
                        pltpu.VMEM((ngrp, n2, n1, FNET_GROUP_DIM), F32)],
        compiler_params=pltpu.CompilerParams(
            dimension_semantics=("arbitrary", "arbitrary"),
            vmem_limit_bytes=VMEM_LIMIT_BYTES),
        name="fnet",
    )(proj3, proj3, w_f, cs_ch, k1m, e2)


def _fnet_constants(s):
    n2 = FNET_N2
    n1 = s // n2
    gd = FNET_GROUP_DIM
    sub = FNET_SUB
    jk = np.outer(np.arange(gd), np.arange(gd)) * (2.0 * np.pi / gd)
    cs_ch = np.concatenate([np.cos(jk), -np.sin(jk)], axis=1)
    rk = np.outer(np.arange(n1), np.arange(n1)) * (2.0 * np.pi / n1)
    eye = np.eye(sub)
    cr, sr = np.kron(np.cos(rk), eye), np.kron(np.sin(rk), eye)
    k1m = np.block([[cr, sr], [-sr, cr]])
    j = np.arange(n1 // sub)[:, None, None, None, None]
    k2 = np.arange(n2)[None, :, None, None, None]
    a = np.arange(sub)[None, None, :, None, None]
    a2 = np.arange(sub)[None, None, None, :, None]
    c = np.arange(n2)[None, None, None, None, :]
    theta = (2.0 * np.pi / s) * ((c * (sub * j + a + n1 * k2)) % s)
    same = (a == a2)
    e2 = np.concatenate([(np.cos(theta) * same).reshape(n1 // sub, n2 * sub, sub * n2),
                         (np.sin(theta) * same).reshape(n1 // sub, n2 * sub, sub * n2)], axis=2)
    return (jnp.asarray(cs_ch, BF16), jnp.asarray(k1m, BF16), jnp.asarray(e2, BF16))


OUTPROJ_TM = 1024


def _outproj_body(x_ref, na_ref, ssd_ref, fn_ref, w_ref, fw_ref, o_ref, *, final):
    acc = _dot(na_ref[...], w_ref[0:NA_WIDTH, :])
    acc = acc + _dot(ssd_ref[...], w_ref[NA_WIDTH:NA_WIDTH + SSD_WIDTH, :])
    acc = acc + _dot(fn_ref[...], w_ref[NA_WIDTH + SSD_WIDTH:D_MIX, :])
    y = x_ref[...] + acc
    if final:
        ms = jnp.mean(y * y, axis=-1, keepdims=True)
        y = y * lax.rsqrt(ms + RMS_EPS) * fw_ref[...]
    o_ref[...] = y


def _outproj(x2, na2, ssd2, fn2, w_out, final_w, final):
    m = x2.shape[0]
    tm = min(OUTPROJ_TM, m)
    row = lambda width: pl.BlockSpec((tm, width), lambda i: (i, 0))
    return pl.pallas_call(
        functools.partial(_outproj_body, final=final),
        grid=(m // tm,),
        in_specs=[
            row(D_MODEL), row(NA_WIDTH), row(SSD_WIDTH), row(FNET_WIDTH),
            pl.BlockSpec((D_MIX, D_MODEL), lambda i: (0, 0)),
            pl.BlockSpec((1, D_MODEL), lambda i: (0, 0)),
        ],
        out_specs=row(D_MODEL),
        out_shape=jax.ShapeDtypeStruct((m, D_MODEL), F32),
        compiler_params=pltpu.CompilerParams(
            dimension_semantics=("arbitrary",),
            vmem_limit_bytes=VMEM_LIMIT_BYTES),
        name="outproj_final" if final else "outproj",
    )(x2, na2, ssd2, fn2, w_out, final_w.reshape(1, D_MODEL))


def _layer_params(w_in, dt_bias, a_log, d_skip, ssd_norm_w):
    dt0 = 4 * NA_WIDTH + SSD_WIDTH + (SSD_WIDTH + 2 * SSD_GROUPS * SSD_STATE)
    w_main = jnp.concatenate([w_in[:, :dt0], w_in[:, dt0 + 2 * SSD_HEADS:]], axis=1).astype(BF16)
    wdt = w_in[:, dt0:dt0 + 2 * SSD_HEADS].reshape(D_MODEL, 2, SSD_GROUPS, SSD_HPG)
    wdt = wdt.transpose(0, 2, 1, 3).reshape(D_MODEL, SSD_GROUPS, 2 * SSD_HPG)
    wdt = jnp.pad(wdt, ((0, 0), (0, 0), (0, LANES - 2 * SSD_HPG))).reshape(D_MODEL, DT_COLS).astype(BF16)

    def head_rows(v):
        v = v.reshape(2, SSD_GROUPS, SSD_HPG).transpose(1, 0, 2).reshape(SSD_GROUPS, 1, 2 * SSD_HPG)
        return jnp.pad(v, ((0, 0), (0, 0), (0, LANES - 2 * SSD_HPG)))

    dskip = jnp.repeat(d_skip, SSD_HEAD_DIM).reshape(SSD_GROUPS, 1, SSD_GW)
    nw = ssd_norm_w.reshape(SSD_GROUPS, 1, SSD_GW)
    return dict(w_main=w_main, w_dt=wdt, dtb=head_rows(dt_bias), alog=head_rows(a_log), dskip=dskip, nw=nw)


def kernel(x, norm_w, w_in, na_rpb, conv_w, conv_b, dt_bias, a_log, d_skip, ssd_norm_w,
           w_fourier, w_out, final_norm_w):
    b, s, d = x.shape
    depth = w_in.shape[0]
    tri, expand = _ssd_constants()
    cs_ch, r1, e2 = _fnet_constants(s)
    x2 = x.reshape(b * s, d)
    for i in range(depth):
        p = _layer_params(w_in[i], dt_bias[i], a_log[i], d_skip[i], ssd_norm_w[i])
        proj, dt = _inproj(x2, s, norm_w[i], p["w_main"], p["w_dt"], conv_w[i], conv_b[i])
        proj3 = proj.reshape(b, s, PROJ_COLS)
        dt3 = dt.reshape(b, s, DT_COLS)
        na = _natten(proj3, _natten_bias_table(na_rpb[i]))
        ssd = _ssd(proj3, dt3, p["dtb"], p["alog"], p["dskip"], p["nw"], tri, expand)
        fn = _fnet(proj3, w_fourier[i].astype(BF16), cs_ch, r1, e2)
        x2 = _outproj(x2, na.reshape(b * s, NA_WIDTH), ssd.reshape(b * s, SSD_WIDTH),
                      fn.reshape(b * s, FNET_WIDTH), w_out[i].astype(BF16), final_norm_w,
                      final=(i == depth - 1))
    return x2.reshape(b, s, d)
```
